```python
import math
import jax, jax.numpy as jnp
from jax import lax
import numpy as np

D_MODEL = 1024
BATCH = 8
SEQ = 2048
DEPTH = 4
DEC_BATCH = 128
DEC_SEQ = 4
PAST_LEN = 16384
PAGE_SIZE = 128

N_MIXERS = 3
N_RET = (DEPTH + 2) // 3
N_SSD = (DEPTH + 1) // 3
N_GDN = DEPTH // 3
RET_HEADS = 4
RET_DK = D_MODEL // RET_HEADS
RET_DV = 2 * RET_DK
RET_QK = RET_HEADS * RET_DK
RET_V = RET_HEADS * RET_DV
RET_IN = 2 * RET_QK + 2 * RET_V
ROPE_BASE = 10000.0
SSD_INNER = 2 * D_MODEL
SSD_HEADDIM = 64
SSD_HEADS = SSD_INNER // SSD_HEADDIM
SSD_GROUPS = 8
SSD_STATE = 128
SSD_CONV_DIM = SSD_INNER + 2 * SSD_GROUPS * SSD_STATE
SSD_IN = SSD_INNER + SSD_CONV_DIM + SSD_HEADS
SSD_NORM_GROUP = SSD_INNER // SSD_GROUPS
GDN_KH = 8
GDN_VH = 16
GDN_DK = 128
GDN_DV = 128
GDN_KD = GDN_KH * GDN_DK
GDN_V = GDN_VH * GDN_DV
GDN_CONV_DIM = 2 * GDN_KD + GDN_V
GDN_IN = GDN_CONV_DIM + GDN_V + 2 * GDN_VH
CONV_W = 4
D_FF = 4 * D_MODEL
CHUNK = 64
RMS_EPS = 1e-6

kernel_name = "hybrid_retention_ssd_gdn_decoder_step"


def _chunk_len(L):
    return CHUNK if L % CHUNK == 0 else L


def _rms(x, w=None):
    xf = x.astype(jnp.float32)
    y = xf * lax.rsqrt(jnp.mean(xf * xf, axis=-1, keepdims=True) + RMS_EPS)
    if w is not None:
        y = y * w.astype(jnp.float32)
    return y.astype(x.dtype)


def _l2norm(x):
    xf = x.astype(jnp.float32)
    return xf * lax.rsqrt(jnp.sum(xf * xf, axis=-1, keepdims=True) + RMS_EPS)


def _causal_conv(u, buf, w):
    L = u.shape[1]
    up = jnp.concatenate([buf.astype(u.dtype), u], axis=1)
    out = up[:, 0:L] * w[0]
    for tap in range(1, CONV_W):
        out = out + up[:, tap:tap + L] * w[tap]
    return out, up[:, L:]


def _rotary(x, pos0):
    L, half = x.shape[1], x.shape[-1] // 2
    inv = ROPE_BASE ** (-jnp.arange(half, dtype=jnp.float32) / half)
    ang = (jnp.arange(L, dtype=jnp.float32) + pos0)[:, None] * inv[None, :]
    cos, sin = jnp.cos(ang)[None, :, None, :], jnp.sin(ang)[None, :, None, :]
    xf = x.astype(jnp.float32)
    x1, x2 = xf[..., :half], xf[..., half:]
    return jnp.concatenate([x1 * cos - x2 * sin, x1 * sin + x2 * cos], axis=-1).astype(x.dtype)


def _decay_scan(q, k, v, log_a, s0):
    Bsz, L = q.shape[:2]
    chunk = _chunk_len(L)
    nc = L // chunk

    def blocks(t):
        return t.astype(jnp.float32).reshape((Bsz, nc, chunk) + t.shape[2:])

    qc, kc, vc = blocks(q), blocks(k), blocks(v)
    ct = jnp.moveaxis(jnp.cumsum(blocks(log_a), axis=2), 2, -1)
    incl = jnp.tril(jnp.ones((chunk, chunk), dtype=bool))
    diff = ct[..., :, None] - ct[..., None, :]
    decay = jnp.where(incl, jnp.exp(jnp.where(incl, diff, 0.0)), 0.0)
    scores = jnp.einsum('bctgn,bcsgn->bcgts', qc, kc)
    y_intra = jnp.einsum('bcgrts,bcsgrp->bctgrp', scores[:, :, :, None] * decay, vc)

    def step(S, inp):
        q_c, k_c, v_c, ct_c = inp
        y_c = jnp.einsum('btgn,bgrnp,bgrt->btgrp', q_c, S, jnp.exp(ct_c))
        w_end = jnp.exp(ct_c[..., -1:] - ct_c)
        S = S * jnp.exp(ct_c[..., -1])[..., None, None] + jnp.einsum('bsgn,bgrs,bsgrp->bgrnp', k_c, w_end, v_c)
        return S, y_c

    xs = tuple(jnp.moveaxis(t, 1, 0) for t in (qc, kc, vc, ct))
    s_fin, y_inter = lax.scan(step, s0.astype(jnp.float32), xs)
    y = y_intra + jnp.moveaxis(y_inter, 0, 1)
    return y.reshape((Bsz, L) + v.shape[2:]), s_fin


def _gated_delta_scan(q, k, v, g, beta, s0):
    Bsz, L, H, K = q.shape
    V = v.shape[-1]
    chunk = _chunk_len(L)
    nc = L // chunk

    def blocks(t):
        return t.astype(jnp.float32).reshape((Bsz, nc, chunk) + t.shape[2:])

    qc, kc, vc, bc = blocks(q), blocks(k), blocks(v), blocks(beta)
    gc = jnp.cumsum(blocks(g), axis=2)
    gt = jnp.moveaxis(gc, 2, -1)
    incl = jnp.tril(jnp.ones((chunk, chunk), dtype=bool))
    strict = jnp.tril(jnp.ones((chunk, chunk), dtype=bool), k=-1)
    diff = gt[..., :, None] - gt[..., None, :]
    decay = jnp.where(incl, jnp.exp(jnp.where(incl, diff, 0.0)), 0.0)
    kb = kc * bc[..., None]
    lmat = jnp.where(strict, jnp.einsum('bcthk,bcshk->bchts', kb, kc) * decay, 0.0)
    rhs = jnp.concatenate([jnp.moveaxis(vc * bc[..., None], 2, 3),
                           jnp.moveaxis(kb * jnp.exp(gc)[..., None], 2, 3)], axis=-1)
    eye = jnp.eye(chunk, dtype=jnp.float32)
    sol = lax.linalg.triangular_solve(eye + lmat, rhs, left_side=True, lower=True, unit_diagonal=True)
    u, w = sol[..., :V], sol[..., V:]
    attn = jnp.einsum('bcthk,bcshk->bchts', qc, kc) * decay
    q_dec = qc * jnp.exp(gc)[..., None]
    k_dec = kc * jnp.exp(gc[:, :, -1:] - gc)[..., None]
    c_dec = jnp.exp(gc[:, :, -1])

    def step(S, inp):
        u_c, w_c, attn_c, qd_c, kd_c, cd_c = inp
        v_new = u_c - jnp.einsum('bhtk,bhkv->bhtv', w_c, S)
        o_c = jnp.einsum('bthk,bhkv->bthv', qd_c, S) + jnp.einsum('bhts,bhsv->bthv', attn_c, v_new)
        S = S * cd_c[..., None, None] + jnp.einsum('bshk,bhsv->bhkv', kd_c, v_new)
        return S, o_c

    xs = tuple(jnp.moveaxis(t, 1, 0) for t in (u, w, attn, q_dec, k_dec, c_dec))
    s_fin, o = lax.scan(step, s0.astype(jnp.float32), xs)
    return jnp.moveaxis(o, 0, 1).reshape(Bsz, L, H, V), s_fin


def _retention(h, pos0, s0, w_in, w_out):
    Bsz, L, _ = h.shape
    q, k, v, g = jnp.split(h @ w_in, [RET_QK, 2 * RET_QK, 2 * RET_QK + RET_V], axis=-1)
    q = _rotary(q.reshape(Bsz, L, RET_HEADS, RET_DK), pos0)
    k = _rotary(k.reshape(Bsz, L, RET_HEADS, RET_DK), pos0) * (RET_DK ** -0.5)
    v = v.reshape(Bsz, L, RET_HEADS, 1, RET_DV)
    log_gamma = jnp.log1p(-jnp.exp2(-5.0 - jnp.arange(RET_HEADS, dtype=jnp.float32)))
    log_a = jnp.broadcast_to(log_gamma[:, None], (Bsz, L, RET_HEADS, 1))
    y, s_new = _decay_scan(q, k, v, log_a, s0[:, :, None])
    y = _rms(y[:, :, :, 0])
    out = (jax.nn.silu(g.astype(jnp.float32)) * y.reshape(Bsz, L, RET_V)).astype(h.dtype) @ w_out
    return out, s_new[:, :, 0].astype(s0.dtype)


def _ssd(h, conv_buf, s0, w_in, conv_w, conv_b, dt_bias, a_log, d_skip, norm_w, w_out):
    Bsz, L, _ = h.shape
    R = SSD_HEADS // SSD_GROUPS
    z, xbc, dt = jnp.split(h @ w_in, [SSD_INNER, SSD_INNER + SSD_CONV_DIM], axis=-1)
    xbc, new_buf = _causal_conv(xbc, conv_buf, conv_w)
    xbc = jax.nn.silu(xbc + conv_b)
    xs, b_in, c_out = jnp.split(xbc, [SSD_INNER, SSD_INNER + SSD_GROUPS * SSD_STATE], axis=-1)
    xs = xs.astype(jnp.float32).reshape(Bsz, L, SSD_GROUPS, R, SSD_HEADDIM)
    b_in = b_in.reshape(Bsz, L, SSD_GROUPS, SSD_STATE)
    c_out = c_out.reshape(Bsz, L, SSD_GROUPS, SSD_STATE)
    dt = jax.nn.softplus(dt.astype(jnp.float32) + dt_bias.astype(jnp.float32)).reshape(Bsz, L, SSD_GROUPS, R)
    log_a = dt * (-jnp.exp(a_log.astype(jnp.float32))).reshape(SSD_GROUPS, R)
    s0g = jnp.swapaxes(s0.reshape(Bsz, SSD_GROUPS, R, SSD_HEADDIM, SSD_STATE), -1, -2)
    y, s_new = _decay_scan(c_out, b_in, xs * dt[..., None], log_a, s0g)
    y = y + d_skip.astype(jnp.float32).reshape(SSD_GROUPS, R, 1) * xs
    y = y.reshape(Bsz, L, SSD_INNER) * jax.nn.silu(z.astype(jnp.float32))
    y = _rms(y.reshape(Bsz, L, SSD_GROUPS, SSD_NORM_GROUP)).reshape(Bsz, L, SSD_INNER) * norm_w
    out = y.astype(h.dtype) @ w_out
    s_new = jnp.swapaxes(s_new, -1, -2).reshape(Bsz, SSD_HEADS, SSD_HEADDIM, SSD_STATE)
    return out, s_new.astype(s0.dtype), new_buf


def _gdn(h, conv_buf, s0, w_in, conv_w, dt_bias, a_log, norm_w, w_out):
    Bsz, L, _ = h.shape
    rep = GDN_VH // GDN_KH
    qkv, z, b, a = jnp.split(h @ w_in, [GDN_CONV_DIM, GDN_CONV_DIM + GDN_V, GDN_CONV_DIM + GDN_V + GDN_VH], axis=-1)
    qkv, new_buf = _causal_conv(qkv, conv_buf, conv_w)
    q, k, v = jnp.split(jax.nn.silu(qkv), [GDN_KD, 2 * GDN_KD], axis=-1)
    q = jnp.repeat(_l2norm(q.reshape(Bsz, L, GDN_KH, GDN_DK)), rep, axis=2) * (GDN_DK ** -0.5)
    k = jnp.repeat(_l2norm(k.reshape(Bsz, L, GDN_KH, GDN_DK)), rep, axis=2)
    v = v.reshape(Bsz, L, GDN_VH, GDN_DV)
    beta = jax.nn.sigmoid(b.astype(jnp.float32))
    g = -jnp.exp(a_log.astype(jnp.float32)) * jax.nn.softplus(a.astype(jnp.float32) + dt_bias.astype(jnp.float32))
    o, s_new = _gated_delta_scan(q, k, v, g, beta, s0)
    o = _rms(o, norm_w) * jax.nn.silu(z.astype(jnp.float32).reshape(Bsz, L, GDN_VH, GDN_DV))
    out = o.reshape(Bsz, L, GDN_V).astype(h.dtype) @ w_out
    return out, s_new.astype(s0.dtype), new_buf


def _sqrelu_mlp(h, w_up, w_down):
    return jnp.square(jax.nn.relu(h @ w_up)) @ w_down


def setup_inputs(seed: int = 0) -> dict:
    key = jax.random.key(seed)
    ks = iter(jax.random.split(key, 40))

    def nrm(shape, scale):
        return scale * jax.random.normal(next(ks), shape, jnp.float32)

    def dense(shape):
        return nrm(shape, shape[-2] ** -0.5)

    def gain(shape):
        return 1.0 + nrm(shape, 0.02)

    def dt_bias(shape):
        uu = jax.random.uniform(next(ks), shape, jnp.float32)
        dt = jnp.exp(uu * (math.log(0.1) - math.log(0.001)) + math.log(0.001))
        return dt + jnp.log(-jnp.expm1(-dt))

    def a_log(shape):
        return jnp.log(jax.random.uniform(next(ks), shape, jnp.float32, 1.0, 16.0))

    return {
        "x_prompt": nrm((BATCH, SEQ, D_MODEL), 1.0),
        "x_sample": nrm((DEC_BATCH, DEC_SEQ, D_MODEL), 1.0),
        "state_ret": nrm((N_RET, DEC_BATCH, RET_HEADS, RET_DK, RET_DV), 0.5),
        "state_ssd": nrm((N_SSD, DEC_BATCH, SSD_HEADS, SSD_HEADDIM, SSD_STATE), 0.1),
        "state_ssd_conv": nrm((N_SSD, DEC_BATCH, CONV_W - 1, SSD_CONV_DIM), 1.0),
        "state_gdn": nrm((N_GDN, DEC_BATCH, GDN_VH, GDN_DK, GDN_DV), 0.3),
        "state_gdn_conv": nrm((N_GDN, DEC_BATCH, CONV_W - 1, GDN_CONV_DIM), 1.0),
        "norm_mix": gain((DEPTH, D_MODEL)),
        "norm_mlp": gain((DEPTH, D_MODEL)),
        "norm_final": gain((D_MODEL,)),
        "ret_w_in": dense((N_RET, D_MODEL, RET_IN)),
        "ret_w_out": dense((N_RET, RET_V, D_MODEL)),
        "ssd_w_in": dense((N_SSD, D_MODEL, SSD_IN)),
        "ssd_conv_w": nrm((N_SSD, CONV_W, SSD_CONV_DIM), CONV_W ** -0.5),
        "ssd_conv_b": nrm((N_SSD, SSD_CONV_DIM), 0.01),
        "ssd_dt_bias": dt_bias((N_SSD, SSD_HEADS)),
        "ssd_a_log": a_log((N_SSD, SSD_HEADS)),
        "ssd_d": gain((N_SSD, SSD_HEADS)),
        "ssd_norm": gain((N_SSD, SSD_INNER)),
        "ssd_w_out": dense((N_SSD, SSD_INNER, D_MODEL)),
        "gdn_w_in": dense((N_GDN, D_MODEL, GDN_IN)),
        "gdn_conv_w": nrm((N_GDN, CONV_W, GDN_CONV_DIM), CONV_W ** -0.5),
        "gdn_dt_bias": dt_bias((N_GDN, GDN_VH)),
        "gdn_a_log": a_log((N_GDN, GDN_VH)),
        "gdn_norm": gain((N_GDN, GDN_DV)),
        "gdn_w_out": dense((N_GDN, GDN_V, D_MODEL)),
        "mlp_w_up": dense((DEPTH, D_MODEL, D_FF)),
        "mlp_w_down": dense((DEPTH, D_FF, D_MODEL)),
    }


def reference(x_prompt, x_sample, state_ret, state_ssd, state_ssd_conv, state_gdn, state_gdn_conv,
              norm_mix, norm_mlp, norm_final, ret_w_in, ret_w_out,
              ssd_w_in, ssd_conv_w, ssd_conv_b, ssd_dt_bias, ssd_a_log, ssd_d, ssd_norm, ssd_w_out,
              gdn_w_in, gdn_conv_w, gdn_dt_bias, gdn_a_log, gdn_norm, gdn_w_out,
              mlp_w_up, mlp_w_down):

    def trunk(x, pos0, s_ret, s_ssd, c_ssd, s_gdn, c_gdn):
        n_ret, n_ssd, n_ssdc, n_gdn, n_gdnc = [], [], [], [], []
        for i in range(DEPTH):
            kind, j = i % N_MIXERS, i // N_MIXERS
            h = _rms(x, norm_mix[i])
            if kind == 0:
                out, s = _retention(h, pos0, s_ret[j], ret_w_in[j], ret_w_out[j])
                n_ret.append(s)
            elif kind == 1:
                out, s, c = _ssd(h, c_ssd[j], s_ssd[j], ssd_w_in[j], ssd_conv_w[j], ssd_conv_b[j],
                                 ssd_dt_bias[j], ssd_a_log[j], ssd_d[j], ssd_norm[j], ssd_w_out[j])
                n_ssd.append(s)
                n_ssdc.append(c)
            else:
                out, s, c = _gdn(h, c_gdn[j], s_gdn[j], gdn_w_in[j], gdn_conv_w[j], gdn_dt_bias[j],
                                 gdn_a_log[j], gdn_norm[j], gdn_w_out[j])
                n_gdn.append(s)
                n_gdnc.append(c)
            x = x + out
            x = x + _sqrelu_mlp(_rms(x, norm_mlp[i]), mlp_w_up[i], mlp_w_down[i])
        return (_rms(x, norm_final), jnp.stack(n_ret), jnp.stack(n_ssd), jnp.stack(n_ssdc),
                jnp.stack(n_gdn), jnp.stack(n_gdnc))

    bp = x_prompt.shape[0]
    dt_p = x_prompt.dtype
    y_prompt, p_ret, p_ssd, p_ssd_conv, p_gdn, p_gdn_conv = trunk(
        x_prompt, 0,
        jnp.zeros((N_RET, bp, RET_HEADS, RET_DK, RET_DV), dt_p),
        jnp.zeros((N_SSD, bp, SSD_HEADS, SSD_HEADDIM, SSD_STATE), dt_p),
        jnp.zeros((N_SSD, bp, CONV_W - 1, SSD_CONV_DIM), dt_p),
        jnp.zeros((N_GDN, bp, GDN_VH, GDN_DK, GDN_DV), dt_p),
        jnp.zeros((N_GDN, bp, CONV_W - 1, GDN_CONV_DIM), dt_p))
    y_sample, s_ret, s_ssd, s_ssd_conv, s_gdn, s_gdn_conv = trunk(
        x_sample, PAST_LEN, state_ret, state_ssd, state_ssd_conv, state_gdn, state_gdn_conv)
    return (y_prompt, y_sample, p_ret, p_ssd, p_ssd_conv, p_gdn, p_gdn_conv,
            s_ret, s_ssd, s_ssd_conv, s_gdn, s_gdn_conv)
```

```python
import functools
import math

import jax
import jax.numpy as jnp
from jax import lax
from jax.experimental import pallas as pl
from jax.experimental.pallas import tpu as pltpu

F32 = jnp.float32
BF16 = jnp.bfloat16

D_MODEL = 1024
RET_HEADS = 4
RET_DK = 256
RET_DV = 512
RET_QK = RET_HEADS * RET_DK
RET_V = RET_HEADS * RET_DV
ROPE_BASE = 10000.0
SSD_INNER = 2048
SSD_HEADDIM = 64
SSD_HEADS = 32
SSD_GROUPS = 8
SSD_STATE = 128
SSD_GROUP_HEADS = SSD_HEADS // SSD_GROUPS
SSD_GROUP_INNER = SSD_INNER // SSD_GROUPS
GDN_KH = 8
GDN_VH = 16
GDN_DK = 128
GDN_DV = 128
GDN_KD = GDN_KH * GDN_DK
GDN_V = GDN_VH * GDN_DV
GDN_REP = GDN_VH // GDN_KH
CONV_W = 4
D_FF = 4096
RMS_EPS = 1e-6
PAST_LEN = 16384

LANES = 128
SUBLANES = 8
VMEM_LIMIT_BYTES = 56 * 1024 * 1024

SAMPLE_TILE = 128
RET_CHUNK = 256
SSD_CHUNK = 128
GDN_CHUNK = 128
HALO = SUBLANES

_NT = (((1,), (1,)), ((), ()))
_TN = (((0,), (0,)), ((), ()))


def _params(*sem):
    return pltpu.CompilerParams(dimension_semantics=sem, vmem_limit_bytes=VMEM_LIMIT_BYTES)


def _dot(a, b):
    return jnp.dot(a, b, preferred_element_type=F32)


def _dot_nt(a, b):
    return lax.dot_general(a, b, _NT, preferred_element_type=F32)


def _dot_tn(a, b):
    return lax.dot_general(a, b, _TN, preferred_element_type=F32)


def _rms_rows(x):
    return x * lax.rsqrt(jnp.mean(x * x, axis=-1, keepdims=True) + RMS_EPS)


def _log2(n):
    k = n.bit_length() - 1
    assert 1 << k == n, n
    return k


def _seq_masks(rows, seq_len):
    t = lax.broadcasted_iota(jnp.int32, (rows, rows), 0)
    s = lax.broadcasted_iota(jnp.int32, (rows, rows), 1)
    sh = _log2(seq_len)
    same = (t >> sh) == (s >> sh)
    return same & (t >= s), same & (t > s)


def _cumsum_rows(x, seq_len):
    pos = lax.broadcasted_iota(jnp.int32, x.shape, 0) & (seq_len - 1)
    k = 1
    while k < seq_len:
        x = x + jnp.where(pos >= k, pltpu.roll(x, k, 0), 0.0)
        k *= 2
    return x


def _proj_kernel(x_ref, nw_ref, w_ref, cos_ref, sin_ref, o_ref, h_ref, *, n_rot, k_scale):
    j = pl.program_id(1)

    @pl.when(j == 0)
    def _():
        h = _rms_rows(x_ref[...]) * nw_ref[...]
        h_ref[...] = h.astype(BF16)

    acc = _dot(h_ref[...], w_ref[...])
    if n_rot == 0:
        o_ref[...] = acc.astype(o_ref.dtype)
        return

    @pl.when(j < n_rot)
    def _():
        c, s = cos_ref[...], sin_ref[...]
        scale = jnp.where(j == 0, 1.0, k_scale).astype(F32)
        half = RET_DK // 2
        for hd in range(acc.shape[1] // RET_DK):
            lo = hd * RET_DK
            x1 = acc[:, lo:lo + half]
            x2 = acc[:, lo + half:lo + RET_DK]
            o_ref[:, lo:lo + half] = ((x1 * c - x2 * s) * scale).astype(o_ref.dtype)
            o_ref[:, lo + half:lo + RET_DK] = ((x1 * s + x2 * c) * scale).astype(o_ref.dtype)

    @pl.when(j >= n_rot)
    def _():
        o_ref[...] = acc.astype(o_ref.dtype)


def _rms_proj(x, norm_w, w_bf16, *, out_dtype, tn, rot=None):
    m, d = x.shape
    n = w_bf16.shape[1]
    tm = min(m, 1024)
    assert m % tm == 0 and n % tn == 0
    if rot is None:
        cos = sin = jnp.zeros((SUBLANES, LANES), F32)
        rot_spec = pl.BlockSpec((SUBLANES, LANES), lambda i, j: (0, 0))
        n_rot = 0
    else:
        cos, sin = rot
        period = cos.shape[0] // tm
        assert cos.shape[0] % tm == 0 and tn == RET_QK
        rot_spec = pl.BlockSpec((tm, LANES), lambda i, j: (i % period, 0))
        n_rot = 2
    kern = functools.partial(_proj_kernel, n_rot=n_rot, k_scale=RET_DK ** -0.5)
    return pl.pallas_call(
        kern,
        grid=(m // tm, n // tn),
        in_specs=[
            pl.BlockSpec((tm, d), lambda i, j: (i, 0)),
            pl.BlockSpec((1, d), lambda i, j: (0, 0)),
            pl.BlockSpec((d, tn), lambda i, j: (0, j)),
            rot_spec, rot_spec,
        ],
        out_specs=pl.BlockSpec((tm, tn), lambda i, j: (i, j)),
        out_shape=jax.ShapeDtypeStruct((m, n), out_dtype),
        scratch_shapes=[pltpu.VMEM((tm, d), BF16)],
        compiler_params=_params("parallel", "arbitrary"),
        name="rms_proj",
    )(x, norm_w.reshape(1, d), w_bf16, cos, sin)


def _out_proj_kernel(y_ref, w_ref, x_ref, o_ref):
    o_ref[...] = x_ref[...] + _dot(y_ref[...], w_ref[...])


def _out_proj(y, w_bf16, x):
    m, k = y.shape
    d = w_bf16.shape[1]
    tm = min(m, 512)
    return pl.pallas_call(
        _out_proj_kernel,
        grid=(m // tm,),
        in_specs=[
            pl.BlockSpec((tm, k), lambda i: (i, 0)),
            pl.BlockSpec((k, d), lambda i: (0, 0)),
            pl.BlockSpec((tm, d), lambda i: (i, 0)),
        ],
        out_specs=pl.BlockSpec((tm, d), lambda i: (i, 0)),
        out_shape=jax.ShapeDtypeStruct((m, d), F32),
        compiler_params=_params("parallel"),
        name="out_proj",
    )(y, w_bf16, x)


def _mlp_kernel(x_ref, nw_ref, wu_ref, wd_ref, fw_ref, o_ref, h_ref, acc_ref, *, final):
    j = pl.program_id(1)

    @pl.when(j == 0)
    def _():
        h_ref[...] = (_rms_rows(x_ref[...]) * nw_ref[...]).astype(BF16)
        acc_ref[...] = jnp.zeros(acc_ref.shape, F32)

    u = _dot(h_ref[...], wu_ref[...])
    a = jnp.square(jnp.maximum(u, 0.0)).astype(BF16)
    acc_ref[...] += _dot(a, wd_ref[...])

    @pl.when(j == pl.num_programs(1) - 1)
    def _():
        y = x_ref[...] + acc_ref[...]
        if final:
            y = _rms_rows(y) * fw_ref[...]
        o_ref[...] = y


def _mlp(x, norm_w, wu_bf16, wd_bf16, final_w=None):
    m, d = x.shape
    ff = wu_bf16.shape[1]
    tm = min(m, 1024)
    tf = 512
    final = final_w is not None
    fw = (final_w if final else jnp.zeros((d,), F32)).reshape(1, d)
    return pl.pallas_call(
        functools.partial(_mlp_kernel, final=final),
        grid=(m // tm, ff // tf),
        in_specs=[
            pl.BlockSpec((tm, d), lambda i, j: (i, 0)),
            pl.BlockSpec((1, d), lambda i, j: (0, 0)),
            pl.BlockSpec((d, tf), lambda i, j: (0, j)),
            pl.BlockSpec((tf, d), lambda i, j: (j, 0)),
            pl.BlockSpec((1, d), lambda i, j: (0, 0)),
        ],
        out_specs=pl.BlockSpec((tm, d), lambda i, j: (i, 0)),
        out_shape=jax.ShapeDtypeStruct((m, d), F32),
        scratch_shapes=[pltpu.VMEM((tm, d), BF16), pltpu.VMEM((tm, d), F32)],
        compiler_params=_params("parallel", "arbitrary"),
        name="mlp",
    )(x, norm_w.reshape(1, d), wu_bf16, wd_bf16, fw)


def _ret_log_gamma(h):
    return math.log1p(-(2.0 ** (-5.0 - h)))


def _ret_decay(rows, seq_len, h):
    lg = _ret_log_gamma(h)
    t = lax.broadcasted_iota(jnp.int32, (rows, rows), 0)
    s = lax.broadcasted_iota(jnp.int32, (rows, rows), 1)
    incl, _ = _seq_masks(rows, seq_len)
    d = jnp.where(incl, jnp.exp(jnp.where(incl, (t - s).astype(F32) * lg, 0.0)), 0.0)
    pos = (lax.broadcasted_iota(jnp.int32, (rows, 1), 0) & (seq_len - 1)).astype(F32)
    return d, pos, lg


def _ret_gate_norm(y, g):
    return jax.nn.silu(g.astype(F32)) * _rms_rows(y)


def _ret_prompt_kernel(q_ref, k_ref, v_ref, g_ref, y_ref, so_ref, s_ref):
    c = pl.program_id(1)
    rows = q_ref.shape[0]

    @pl.when(c == 0)
    def _():
        s_ref[...] = jnp.zeros(s_ref.shape, F32)

    for h in range(RET_HEADS):
        d, pos, lg = _ret_decay(rows, rows, h)
        q = q_ref[:, h * RET_DK:(h + 1) * RET_DK]
        k = k_ref[:, h * RET_DK:(h + 1) * RET_DK]
        v = v_ref[:, h * RET_DV:(h + 1) * RET_DV]
        a = (_dot_nt(q, k) * d).astype(BF16)
        s_old = s_ref[h]
        y = _dot(a, v) + jnp.exp((pos + 1.0) * lg) * _dot(q, s_old.astype(BF16))
        kw = (k.astype(F32) * jnp.exp((rows - 1.0 - pos) * lg)).astype(BF16)
        s_ref[h] = s_old * math.exp(rows * lg) + _dot_tn(kw, v)
        y_ref[:, h * RET_DV:(h + 1) * RET_DV] = _ret_gate_norm(
            y, g_ref[:, h * RET_DV:(h + 1) * RET_DV]).astype(y_ref.dtype)

    @pl.when(c == pl.num_programs(1) - 1)
    def _():
        so_ref[0] = s_ref[...]


def _ret_prompt(proj, batch, seq):
    t = min(RET_CHUNK, seq)
    nc = seq // t
    row = lambda b, c: b * nc + c
    return pl.pallas_call(
        _ret_prompt_kernel,
        grid=(batch, nc),
        in_specs=[
            pl.BlockSpec((t, RET_QK), lambda b, c: (row(b, c), 0)),
            pl.BlockSpec((t, RET_QK), lambda b, c: (row(b, c), 1)),
            pl.BlockSpec((t, RET_V), lambda b, c: (row(b, c), 1)),
            pl.BlockSpec((t, RET_V), lambda b, c: (row(b, c), 2)),
        ],
        out_specs=[
            pl.BlockSpec((t, RET_V), lambda b, c: (row(b, c), 0)),
            pl.BlockSpec((1, RET_HEADS, RET_DK, RET_DV), lambda b, c: (b, 0, 0, 0)),
        ],
        out_shape=[
            jax.ShapeDtypeStruct((batch * seq, RET_V), BF16),
            jax.ShapeDtypeStruct((batch, RET_HEADS, RET_DK, RET_DV), F32),
        ],
        scratch_shapes=[pltpu.VMEM((RET_HEADS, RET_DK, RET_DV), F32)],
        compiler_params=_params("parallel", "arbitrary"),
        name="ret_prompt",
    )(proj, proj, proj, proj)


def _ret_sample_kernel(q_ref, k_ref, v_ref, g_ref, s0_ref, y_ref, so_ref, acc_ref, *, seq_len):
    j = pl.program_id(1)
    rows = q_ref.shape[0]
    row_seq = lax.broadcasted_iota(jnp.int32, (rows, 1), 0) >> _log2(seq_len)
    mine = row_seq == j

    for h in range(RET_HEADS):
        d, pos, lg = _ret_decay(rows, seq_len, h)
        q = q_ref[:, h * RET_DK:(h + 1) * RET_DK].astype(BF16)
        k = k_ref[:, h * RET_DK:(h + 1) * RET_DK]
        v = v_ref[:, h * RET_DV:(h + 1) * RET_DV].astype(BF16)
        cols = slice(h * RET_DV, (h + 1) * RET_DV)

        @pl.when(j == 0)
        def _():
            a = (_dot_nt(q, k.astype(BF16)) * d).astype(BF16)
            acc_ref[:, cols] = _dot(a, v)

        s_old = s0_ref[0, h]
        y_state = jnp.exp((pos + 1.0) * lg) * _dot(q, s_old.astype(BF16))
        acc_ref[:, cols] += jnp.where(mine, y_state, 0.0)
        kw = jnp.where(mine, k.astype(F32) * jnp.exp((seq_len - 1.0 - pos) * lg), 0.0).astype(BF16)
        so_ref[0, h] = s_old * math.exp(seq_len * lg) + _dot_tn(kw, v)

    @pl.when(j == pl.num_programs(1) - 1)
    def _():
        for h in range(RET_HEADS):
            cols = slice(h * RET_DV, (h + 1) * RET_DV)
            y_ref[:, cols] = _ret_gate_norm(acc_ref[:, cols], g_ref[:, cols]).astype(y_ref.dtype)


def _ret_sample(proj, s0, seq_len):
    m = proj.shape[0]
    rows = SAMPLE_TILE
    per_tile = rows // seq_len
    state_spec = pl.BlockSpec((1, RET_HEADS, RET_DK, RET_DV), lambda i, j: (i * per_tile + j, 0, 0, 0))
    return pl.pallas_call(
        functools.partial(_ret_sample_kernel, seq_len=seq_len),
        grid=(m // rows, per_tile),
        in_specs=[
            pl.BlockSpec((rows, RET_QK), lambda i, j: (i, 0)),
            pl.BlockSpec((rows, RET_QK), lambda i, j: (i, 1)),
            pl.BlockSpec((rows, RET_V), lambda i, j: (i, 1)),
            pl.BlockSpec((rows, RET_V), lambda i, j: (i, 2)),
            state_spec,
        ],
        out_specs=[pl.BlockSpec((rows, RET_V), lambda i, j: (i, 0)), state_spec],
        out_shape=[
            jax.ShapeDtypeStruct((m, RET_V), BF16),
            jax.ShapeDtypeStruct(s0.shape, F32),
        ],
        scratch_shapes=[pltpu.VMEM((rows, RET_V), F32)],
        compiler_params=_params("parallel", "arbitrary"),
        name="ret_sample",
    )(proj, proj, proj, proj, s0)


def _rope_tables(seq, pos0, reps):
    half = RET_DK // 2
    inv = ROPE_BASE ** (-jnp.arange(half, dtype=F32) / half)
    ang = (jnp.arange(seq, dtype=F32) + pos0)[:, None] * inv[None, :]
    return jnp.tile(jnp.cos(ang), (reps, 1)), jnp.tile(jnp.sin(ang), (reps, 1))


def _ret_layer(x, s0, norm_w, w_in, w_out, *, batch, seq, pos0, sample):
    m = x.shape[0]
    if sample:
        rot = _rope_tables(seq, pos0, min(m, 1024) // seq)
    else:
        rot = _rope_tables(seq, pos0, 1)
    proj = _rms_proj(x, norm_w, w_in, out_dtype=F32 if sample else BF16, tn=RET_QK, rot=rot)
    if sample:
        y, s_new = _ret_sample(proj, s0, seq)
    else:
        y, s_new = _ret_prompt(proj, batch, seq)
    return _out_proj(y, w_out, x), s_new


def _conv_chunk(xe_ref, cols, cw_ref, rows):
    base = HALO - (CONV_W - 1)
    acc = xe_ref[pl.ds(base, rows), cols] * cw_ref[0:1, cols]
    for tap in range(1, CONV_W):
        acc = acc + xe_ref[pl.ds(base + tap, rows), cols] * cw_ref[tap:tap + 1, cols]
    return acc


def _conv_packed(u, hist, cw_ref, cols, seq_len):
    rows = u.shape[0]
    pos = lax.broadcasted_iota(jnp.int32, (rows, 1), 0) & (seq_len - 1)
    acc = None
    for tap in range(CONV_W):
        back = CONV_W - 1 - tap
        cur = u if back == 0 else pltpu.roll(u, back, 0)
        if tap < CONV_W - 1:
            old = hist if tap == 0 else pltpu.roll(hist, rows - tap, 0)
            cur = jnp.where(pos + tap <= CONV_W - 2, old, cur)
        term = cur * cw_ref[tap:tap + 1, cols]
        acc = term if acc is None else acc + term
    return acc


def _last_of_seq(x, seq_len):
    rows = x.shape[0]
    if seq_len == rows:
        return jnp.broadcast_to(x[rows - 1:rows, :], x.shape)
    pos = lax.broadcasted_iota(jnp.int32, x.shape, 0) & (seq_len - 1)
    out = x
    for ahead in range(1, seq_len):
        out = jnp.where(pos == seq_len - 1 - ahead, pltpu.roll(x, rows - ahead, 0), out)
    return out


def _pad_lanes(v, offset=0):
    return jnp.pad(v.astype(F32), (offset, LANES - offset - v.shape[0])).reshape(1, LANES)


def _packed_history(buf, seq_len):
    b, r, c = buf.shape
    return jnp.pad(buf, ((0, 0), (0, seq_len - r), (0, 0))).reshape(b * seq_len, c)


def _ssd_expand(x, g):
    rows = x.shape[0]
    head = lax.broadcasted_iota(jnp.int32, (rows, SSD_GROUP_INNER), 1) >> _log2(SSD_HEADDIM)
    h0 = g * SSD_GROUP_HEADS
    out = jnp.broadcast_to(x[:, h0:h0 + 1], (rows, SSD_GROUP_INNER))
    for r in range(1, SSD_GROUP_HEADS):
        out = jnp.where(head == r, jnp.broadcast_to(x[:, h0 + r:h0 + r + 1], (rows, SSD_GROUP_INNER)), out)
    return out


def _ssd_step_terms(dt_ref, dtb_ref, alog_ref, seq_len):
    dt = jax.nn.softplus(dt_ref[...] + dtb_ref[...])
    la = dt * (-jnp.exp(alog_ref[...]))
    ct = _cumsum_rows(la, seq_len)
    return dt, ct


def _ssd_group_intra(g, xs, bg, cg, dt, ct, ct_t, incl):
    rows = xs.shape[0]
    scores = _dot_nt(cg, bg)
    xdt = xs * _ssd_expand(dt, g)
    head = lax.broadcasted_iota(jnp.int32, (rows, SSD_GROUP_INNER), 1) >> _log2(SSD_HEADDIM)
    y = jnp.zeros((rows, SSD_GROUP_INNER), F32)
    for r in range(SSD_GROUP_HEADS):
        h = g * SSD_GROUP_HEADS + r
        diff = ct[:, h:h + 1] - ct_t[h:h + 1, :]
        d = jnp.where(incl, jnp.exp(jnp.where(incl, diff, 0.0)), 0.0)
        a = (scores * d).astype(BF16)
        y = y + _dot(a, jnp.where(head == r, xdt, 0.0).astype(BF16))
    return y, xdt, _ssd_expand(ct, g)


def _ssd_gate_norm(y, z, nw):
    return _rms_rows(y * jax.nn.silu(z.astype(F32))) * nw


def _ssd_prompt_kernel(z_ref, x_ref, bc_ref, dt_ref, cwx_ref, cwbc_ref, cbx_ref, cbbc_ref, dtb_ref, alog_ref,
                       dsk_ref, nw_ref, y_ref, so_ref, xex_ref, xebc_ref, s_ref):
    c = pl.program_id(1)
    rows = x_ref.shape[0]
    gs = SSD_GROUPS * SSD_STATE

    @pl.when(c == 0)
    def _():
        xex_ref[0:HALO, :] = jnp.zeros((HALO, xex_ref.shape[1]), F32)
        xebc_ref[0:HALO, :] = jnp.zeros((HALO, xebc_ref.shape[1]), F32)
        s_ref[...] = jnp.zeros(s_ref.shape, F32)

    xex_ref[HALO:, :] = x_ref[...].astype(F32)
    xebc_ref[HALO:, :] = bc_ref[...].astype(F32)
    dt, ct = _ssd_step_terms(dt_ref, dtb_ref, alog_ref, rows)
    ct_t = ct.T
    incl, _ = _seq_masks(rows, rows)

    for g in range(SSD_GROUPS):
        ci = slice(g * SSD_GROUP_INNER, (g + 1) * SSD_GROUP_INNER)
        cb_ = slice(g * SSD_STATE, (g + 1) * SSD_STATE)
        cc_ = slice(gs + g * SSD_STATE, gs + (g + 1) * SSD_STATE)
        xs = jax.nn.silu(_conv_chunk(xex_ref, ci, cwx_ref, rows) + cbx_ref[:, ci])
        bg = jax.nn.silu(_conv_chunk(xebc_ref, cb_, cwbc_ref, rows) + cbbc_ref[:, cb_]).astype(BF16)
        cg = jax.nn.silu(_conv_chunk(xebc_ref, cc_, cwbc_ref, rows) + cbbc_ref[:, cc_]).astype(BF16)
        y, xdt, ctc = _ssd_group_intra(g, xs, bg, cg, dt, ct, ct_t, incl)
        hs = slice(g * SSD_GROUP_HEADS, (g + 1) * SSD_GROUP_HEADS)
        sg = s_ref[hs].reshape(SSD_GROUP_INNER, SSD_STATE)
        y = y + jnp.exp(ctc) * _dot_nt(cg, sg.astype(BF16)) + dsk_ref[:, ci] * xs
        w_end = jnp.exp(ctc[rows - 1:rows, :] - ctc)
        upd = _dot_tn((xdt * w_end).astype(BF16), bg)
        for r in range(SSD_GROUP_HEADS):
            h = g * SSD_GROUP_HEADS + r
            keep = jnp.exp(ct[rows - 1:rows, h:h + 1])
            s_ref[h] = s_ref[h] * keep + upd[r * SSD_HEADDIM:(r + 1) * SSD_HEADDIM, :]
        y_ref[:, ci] = _ssd_gate_norm(y, z_ref[:, ci], nw_ref[:, ci]).astype(y_ref.dtype)

    xex_ref[0:HALO, :] = xex_ref[rows:rows + HALO, :]
    xebc_ref[0:HALO, :] = xebc_ref[rows:rows + HALO, :]

    @pl.when(c == pl.num_programs(1) - 1)
    def _():
        so_ref[0] = s_ref[...]


def _ssd_prompt(proj, dtp, prm, batch, seq):
    t = min(SSD_CHUNK, seq)
    nc = seq // t
    row = lambda b, c: b * nc + c
    half = SSD_INNER
    full = lambda shape: pl.BlockSpec(shape, lambda b, c: (0,) * len(shape))
    return pl.pallas_call(
        _ssd_prompt_kernel,
        grid=(batch, nc),
        in_specs=[
            pl.BlockSpec((t, half), lambda b, c: (row(b, c), 0)),
            pl.BlockSpec((t, half), lambda b, c: (row(b, c), 1)),
            pl.BlockSpec((t, half), lambda b, c: (row(b, c), 2)),
            pl.BlockSpec((t, LANES), lambda b, c: (row(b, c), 0)),
            pl.BlockSpec((CONV_W, half), lambda b, c: (0, 0)),
            pl.BlockSpec((CONV_W, half), lambda b, c: (0, 1)),
            pl.BlockSpec((1, half), lambda b, c: (0, 0)),
            pl.BlockSpec((1, half), lambda b, c: (0, 1)),
            full((1, LANES)), full((1, LANES)), full((1, half)), full((1, half)),
        ],
        out_specs=[
            pl.BlockSpec((t, half), lambda b, c: (row(b, c), 0)),
            pl.BlockSpec((1, SSD_HEADS, SSD_HEADDIM, SSD_STATE), lambda b, c: (b, 0, 0, 0)),
        ],
        out_shape=[
            jax.ShapeDtypeStruct((batch * seq, half), BF16),
            jax.ShapeDtypeStruct((batch, SSD_HEADS, SSD_HEADDIM, SSD_STATE), F32),
        ],
        scratch_shapes=[
            pltpu.VMEM((t + HALO, half), F32),
            pltpu.VMEM((t + HALO, half), F32),
            pltpu.VMEM((SSD_HEADS, SSD_HEADDIM, SSD_STATE), F32),
        ],
        compiler_params=_params("parallel", "arbitrary"),
        name="ssd_prompt",
    )(proj, proj, proj, dtp, prm["cw"], prm["cw"], prm["cb"], prm["cb"], prm["dtb"], prm["alog"],
      prm["dsk"], prm["nw"])


SSD_SAMPLE_SEQS = 4


def _ssd_sample_kernel(z_ref, x_ref, bc_ref, hx_ref, hbc_ref, dt_ref, cwx_ref, cwbc_ref, cbx_ref, cbbc_ref,
                       dtb_ref, alog_ref, dsk_ref, nw_ref, s0_ref, y_ref, so_ref,
                       acc_ref, ect_ref, xdw_ref, b_ref, c_ref, ct_ref, *, seq_len):
    j = pl.program_id(1)
    rows = x_ref.shape[0]
    gs = SSD_GROUPS * SSD_STATE

    @pl.when(j == 0)
    def _():
        dt, ct = _ssd_step_terms(dt_ref, dtb_ref, alog_ref, seq_len)
        ct_ref[...] = ct
        ct_t = ct.T
        incl, _ = _seq_masks(rows, seq_len)
        for g in range(SSD_GROUPS):
            ci = slice(g * SSD_GROUP_INNER, (g + 1) * SSD_GROUP_INNER)
            cb_ = slice(g * SSD_STATE, (g + 1) * SSD_STATE)
            cc_ = slice(gs + g * SSD_STATE, gs + (g + 1) * SSD_STATE)
            xs = jax.nn.silu(_conv_packed(x_ref[:, ci], hx_ref[:, ci], cwx_ref, ci, seq_len) + cbx_ref[:, ci])
            bg = jax.nn.silu(_conv_packed(bc_ref[:, cb_], hbc_ref[:, cb_], cwbc_ref, cb_, seq_len)
                             + cbbc_ref[:, cb_]).astype(BF16)
            cg = jax.nn.silu(_conv_packed(bc_ref[:, cc_], hbc_ref[:, cc_], cwbc_ref, cc_, seq_len)
                             + cbbc_ref[:, cc_]).astype(BF16)
            y, xdt, ctc = _ssd_group_intra(g, xs, bg, cg, dt, ct, ct_t, incl)
            acc_ref[:, ci] = y + dsk_ref[:, ci] * xs
            ect_ref[:, ci] = jnp.exp(ctc)
            xdw_ref[:, ci] = xdt * jnp.exp(_last_of_seq(ctc, seq_len) - ctc)
            b_ref[:, cb_] = bg
            c_ref[:, cb_] = cg

    row_seq = lax.broadcasted_iota(jnp.int32, (rows, 1), 0) >> _log2(seq_len)
    for n in range(SSD_SAMPLE_SEQS):
        seq = j * SSD_SAMPLE_SEQS + n
        mine = row_seq == seq
        last_row = seq * seq_len + (seq_len - 1)
        for g in range(SSD_GROUPS):
            ci = slice(g * SSD_GROUP_INNER, (g + 1) * SSD_GROUP_INNER)
            cb_ = slice(g * SSD_STATE, (g + 1) * SSD_STATE)
            hs = slice(g * SSD_GROUP_HEADS, (g + 1) * SSD_GROUP_HEADS)
            sg = s0_ref[n, hs].reshape(SSD_GROUP_INNER, SSD_STATE)
            y_state = ect_ref[:, ci] * _dot_nt(c_ref[:, cb_], sg.astype(BF16))
            acc_ref[:, ci] += jnp.where(mine, y_state, 0.0)
            upd = _dot_tn(jnp.where(mine, xdw_ref[:, ci], 0.0).astype(BF16), b_ref[:, cb_])
            for r in range(SSD_GROUP_HEADS):
                h = g * SSD_GROUP_HEADS + r
                keep = jnp.exp(ct_ref[pl.ds(last_row, 1), pl.ds(h, 1)])
                so_ref[n, h] = s0_ref[n, h] * keep + upd[r * SSD_HEADDIM:(r + 1) * SSD_HEADDIM, :]

    @pl.when(j == pl.num_programs(1) - 1)
    def _():
        for g in range(SSD_GROUPS):
            ci = slice(g * SSD_GROUP_INNER, (g + 1) * SSD_GROUP_INNER)
            y_ref[:, ci] = _ssd_gate_norm(acc_ref[:, ci], z_ref[:, ci], nw_ref[:, ci]).astype(y_ref.dtype)


def _ssd_sample(proj, dtp, hist, prm, s0, seq_len):
    m = proj.shape[0]
    rows = SAMPLE_TILE
    nb = SSD_SAMPLE_SEQS
    steps = rows // seq_len // nb
    half = SSD_INNER
    full = lambda shape: pl.BlockSpec(shape, lambda i, j: (0,) * len(shape))
    state_spec = pl.BlockSpec((nb, SSD_HEADS, SSD_HEADDIM, SSD_STATE), lambda i, j: (i * steps + j, 0, 0, 0))
    return pl.pallas_call(
        functools.partial(_ssd_sample_kernel, seq_len=seq_len),
        grid=(m // rows, steps),
        in_specs=[
            pl.BlockSpec((rows, half), lambda i, j: (i, 0)),
            pl.BlockSpec((rows, half), lambda i, j: (i, 1)),
            pl.BlockSpec((rows, half), lambda i, j: (i, 2)),
            pl.BlockSpec((rows, half), lambda i, j: (i, 0)),
            pl.BlockSpec((rows, half), lambda i, j: (i, 1)),
            pl.BlockSpec((rows, LANES), lambda i, j: (i, 0)),
            pl.BlockSpec((CONV_W, half), lambda i, j: (0, 0)),
            pl.BlockSpec((CONV_W, half), lambda i, j: (0, 1)),
            pl.BlockSpec((1, half), lambda i, j: (0, 0)),
            pl.BlockSpec((1, half), lambda i, j: (0, 1)),
            full((1, LANES)), full((1, LANES)), full((1, half)), full((1, half)),
            state_spec,
        ],
        out_specs=[pl.BlockSpec((rows, half), lambda i, j: (i, 0)), state_spec],
        out_shape=[
            jax.ShapeDtypeStruct((m, half), BF16),
            jax.ShapeDtypeStruct(s0.shape, F32),
        ],
        scratch_shapes=[
            pltpu.VMEM((rows, half), F32),
            pltpu.VMEM((rows, half), F32),
            pltpu.VMEM((rows, half), F32),
            pltpu.VMEM((rows, SSD_GROUPS * SSD_STATE), BF16),
            pltpu.VMEM((rows, SSD_GROUPS * SSD_STATE), BF16),
            pltpu.VMEM((rows, LANES), F32),
        ],
        compiler_params=_params("parallel", "arbitrary"),
        name="ssd_sample",
    )(proj, proj, proj, hist, hist, dtp, prm["cw"], prm["cw"], prm["cb"], prm["cb"], prm["dtb"], prm["alog"],
      prm["dsk"], prm["nw"], s0)


def _ssd_layer(x, s0, cbuf, norm_w, w_main, w_dt, prm, w_out, *, batch, seq, sample):
    proj = _rms_proj(x, norm_w, w_main, out_dtype=F32 if sample else BF16, tn=1024)
    dtp = _rms_proj(x, norm_w, w_dt, out_dtype=F32, tn=LANES)
    conv_cols = proj[:, SSD_INNER:].reshape(batch, seq, -1)
    conv_new = conv_cols[:, seq - (CONV_W - 1):, :].astype(F32)
    if sample:
        y, s_new = _ssd_sample(proj, dtp, _packed_history(cbuf, seq), prm, s0, seq)
    else:
        y, s_new = _ssd_prompt(proj, dtp, prm, batch, seq)
    return _out_proj(y, w_out, x), s_new, conv_new


def _l2norm_rows(x):
    return x * lax.rsqrt(jnp.sum(x * x, axis=-1, keepdims=True) + RMS_EPS)


def _gdn_gates(ba_ref, dtb_ref, alog_ref, seq_len):
    ba = ba_ref[...]
    beta = jax.nn.sigmoid(ba)
    g = -jnp.exp(alog_ref[...]) * jax.nn.softplus(ba + dtb_ref[...])
    gc = _cumsum_rows(g, seq_len)
    return beta, gc, _last_of_seq(gc, seq_len)


INV_BASE_BLOCK = 16


def _unit_lower_inverse(l, seq_len):
    rows = l.shape[0]
    ti = lax.broadcasted_iota(jnp.int32, (rows, rows), 0)
    si = lax.broadcasted_iota(jnp.int32, (rows, rows), 1)
    base = min(INV_BASE_BLOCK, seq_len)
    n = jnp.where((ti >> _log2(base)) == (si >> _log2(base)), -l, 0.0)
    x = jnp.where(ti == si, 1.0, 0.0) + n
    for _ in range(_log2(base) - 1):
        n16 = n.astype(BF16)
        n = _dot(n16, n16)
        x = x + _dot(x.astype(BF16), n.astype(BF16))
    blk = base
    while blk < seq_len:
        lower_left = ((ti >> _log2(2 * blk)) == (si >> _log2(2 * blk))) & ((ti >> _log2(blk)) != (si >> _log2(blk)))
        c = jnp.where(lower_left, l, 0.0).astype(BF16)
        x16 = x.astype(BF16)
        x = x - _dot(_dot(x16, c).astype(BF16), x16)
        blk *= 2
    return x


def _gdn_head_intra(q, k, kk, qk, v, beta_all, gc_all, gl_all, gc_t, head, incl, strict, seq_len):
    rows = q.shape[0]
    lane = lax.broadcasted_iota(jnp.int32, (rows, LANES), 1)
    sub = lax.broadcasted_iota(jnp.int32, (LANES, rows), 0)
    beta = jnp.sum(jnp.where(lane == head, beta_all, 0.0), axis=1, keepdims=True)
    gcc = jnp.sum(jnp.where(lane == GDN_VH + head, gc_all, 0.0), axis=1, keepdims=True)
    gcl = jnp.sum(jnp.where(lane == GDN_VH + head, gl_all, 0.0), axis=1, keepdims=True)
    gcr = jnp.sum(jnp.where(sub == GDN_VH + head, gc_t, 0.0), axis=0, keepdims=True)
    d = jnp.where(incl, jnp.exp(jnp.where(incl, gcc - gcr, 0.0)), 0.0)
    p = _unit_lower_inverse(jnp.where(strict, (beta * kk) * d, 0.0), seq_len)
    egc = jnp.exp(gcc)
    rhs = jnp.concatenate([v * beta, k * (beta * egc)], axis=1).astype(BF16)
    sol = _dot(p.astype(BF16), rhs)
    u, w = sol[:, :GDN_DV], sol[:, GDN_DV:]
    attn = (qk * d).astype(BF16)
    qd = (q * egc).astype(BF16)
    kd = (k * jnp.exp(gcl - gcc)).astype(BF16)
    return u, w.astype(BF16), attn, qd, kd, gcc


def _gdn_gate_norm(o, z, nw):
    return _rms_rows(o) * nw * jax.nn.silu(z.astype(F32))


def _gdn_prompt_kernel(q_ref, k_ref, v_ref, z_ref, ba_ref, cwq_ref, cwk_ref, cwv_ref, dtb_ref, alog_ref, nw_ref,
                       o_ref, so_ref, xeq_ref, xek_ref, xev_ref, s_ref):
    kh = pl.program_id(1)
    c = pl.program_id(2)
    rows = q_ref.shape[0]

    @pl.when(c == 0)
    def _():
        for xe in (xeq_ref, xek_ref, xev_ref):
            xe[0:HALO, :] = jnp.zeros((HALO, xe.shape[1]), F32)
        s_ref[...] = jnp.zeros(s_ref.shape, F32)

    xeq_ref[HALO:, :] = q_ref[...].astype(F32)
    xek_ref[HALO:, :] = k_ref[...].astype(F32)
    xev_ref[HALO:, :] = v_ref[...].astype(F32)
    q = _l2norm_rows(jax.nn.silu(_conv_chunk(xeq_ref, slice(0, GDN_DK), cwq_ref, rows))) * (GDN_DK ** -0.5)
    k = _l2norm_rows(jax.nn.silu(_conv_chunk(xek_ref, slice(0, GDN_DK), cwk_ref, rows)))
    v_all = jax.nn.silu(_conv_chunk(xev_ref, slice(0, GDN_REP * GDN_DV), cwv_ref, rows))
    for xe in (xeq_ref, xek_ref, xev_ref):
        xe[0:HALO, :] = xe[rows:rows + HALO, :]

    beta_all, gc_all, gl_all = _gdn_gates(ba_ref, dtb_ref, alog_ref, rows)
    gc_t = gc_all.T
    incl, strict = _seq_masks(rows, rows)
    q16, k16 = q.astype(BF16), k.astype(BF16)
    kk = _dot_nt(k16, k16)
    qk = _dot_nt(q16, k16)

    for r in range(GDN_REP):
        cv = slice(r * GDN_DV, (r + 1) * GDN_DV)
        u, w16, attn, qd, kd, gcc = _gdn_head_intra(
            q, k, kk, qk, v_all[:, cv], beta_all, gc_all, gl_all, gc_t, kh * GDN_REP + r, incl, strict, rows)
        s_old = s_ref[r]
        s16 = s_old.astype(BF16)
        v_new = u - _dot(w16, s16)
        o = _dot(qd, s16) + _dot(attn, v_new.astype(BF16))
        s_ref[r] = s_old * jnp.exp(gcc[rows - 1:rows, :]) + _dot_tn(kd, v_new.astype(BF16))
        o_ref[:, cv] = _gdn_gate_norm(o, z_ref[:, cv], nw_ref[...]).astype(o_ref.dtype)

    @pl.when(c == pl.num_programs(2) - 1)
    def _():
        so_ref[0] = s_ref[...]


def _gdn_col_specs(rows, row_map):
    kcol = GDN_KD // GDN_DK
    vcol = 2 * GDN_KD // (GDN_REP * GDN_DV)
    return [
        pl.BlockSpec((rows, GDN_DK), lambda *ix: (row_map(*ix), ix[1])),
        pl.BlockSpec((rows, GDN_DK), lambda *ix: (row_map(*ix), kcol + ix[1])),
        pl.BlockSpec((rows, GDN_REP * GDN_DV), lambda *ix: (row_map(*ix), vcol + ix[1])),
    ]


def _gdn_prompt(proj, ba, prm, batch, seq):
    t = min(GDN_CHUNK, seq)
    nc = seq // t
    row_map = lambda b, kh, c: b * nc + c
    zcol = (2 * GDN_KD + GDN_V) // (GDN_REP * GDN_DV)
    full = lambda shape: pl.BlockSpec(shape, lambda b, kh, c: (0,) * len(shape))
    pair = GDN_REP * GDN_DV
    return pl.pallas_call(
        _gdn_prompt_kernel,
        grid=(batch, GDN_KH, nc),
        in_specs=_gdn_col_specs(t, row_map) + [
            pl.BlockSpec((t, pair), lambda b, kh, c: (row_map(b, kh, c), zcol + kh)),
            pl.BlockSpec((t, LANES), lambda b, kh, c: (row_map(b, kh, c), 0)),
        ] + _gdn_col_specs(CONV_W, lambda b, kh, c: 0) + [full((1, LANES)), full((1, LANES)), full((1, GDN_DV))],
        out_specs=[
            pl.BlockSpec((t, pair), lambda b, kh, c: (row_map(b, kh, c), kh)),
            pl.BlockSpec((1, GDN_REP, GDN_DK, GDN_DV), lambda b, kh, c: (b, kh, 0, 0)),
        ],
        out_shape=[
            jax.ShapeDtypeStruct((batch * seq, GDN_V), BF16),
            jax.ShapeDtypeStruct((batch, GDN_VH, GDN_DK, GDN_DV), F32),
        ],
        scratch_shapes=[
            pltpu.VMEM((t + HALO, GDN_DK), F32),
            pltpu.VMEM((t + HALO, GDN_DK), F32),
            pltpu.VMEM((t + HALO, pair), F32),
            pltpu.VMEM((GDN_REP, GDN_DK, GDN_DV), F32),
        ],
        compiler_params=_params("parallel", "parallel", "arbitrary"),
        name="gdn_prompt",
    )(proj, proj, proj, proj, ba, prm["cw"], prm["cw"], prm["cw"], prm["dtb"], prm["alog"], prm["nw"])


GDN_SAMPLE_SEQS = 8


def _gdn_sample_kernel(q_ref, k_ref, v_ref, z_ref, ba_ref, hq_ref, hk_ref, hv_ref, cwq_ref, cwk_ref, cwv_ref,
                       dtb_ref, alog_ref, nw_ref, s0_ref, o_ref, so_ref,
                       u_ref, w_ref, attn_ref, qd_ref, kd_ref, gcc_ref, vnew_ref, qs_ref, *, seq_len):
    kh = pl.program_id(1)
    j = pl.program_id(2)
    rows = q_ref.shape[0]

    @pl.when(j == 0)
    def _():
        one = slice(0, GDN_DK)
        q = _l2norm_rows(jax.nn.silu(_conv_packed(q_ref[...], hq_ref[...], cwq_ref, one, seq_len))) * (GDN_DK ** -0.5)
        k = _l2norm_rows(jax.nn.silu(_conv_packed(k_ref[...], hk_ref[...], cwk_ref, one, seq_len)))
        v_all = jax.nn.silu(_conv_packed(v_ref[...], hv_ref[...], cwv_ref, slice(0, GDN_REP * GDN_DV), seq_len))
        beta_all, gc_all, gl_all = _gdn_gates(ba_ref, dtb_ref, alog_ref, seq_len)
        gc_t = gc_all.T
        incl, strict = _seq_masks(rows, seq_len)
        q16, k16 = q.astype(BF16), k.astype(BF16)
        kk = _dot_nt(k16, k16)
        qk = _dot_nt(q16, k16)
        vnew_ref[...] = jnp.zeros(vnew_ref.shape, F32)
        qs_ref[...] = jnp.zeros(qs_ref.shape, F32)
        for r in range(GDN_REP):
            cv = slice(r * GDN_DV, (r + 1) * GDN_DV)
            u, w16, attn, qd, kd, gcc = _gdn_head_intra(
                q, k, kk, qk, v_all[:, cv], beta_all, gc_all, gl_all, gc_t, kh * GDN_REP + r, incl, strict,
                seq_len)
            u_ref[r] = u
            w_ref[r] = w16
            attn_ref[r] = attn
            qd_ref[r] = qd
            kd_ref[r] = kd
            gcc_ref[r] = jnp.broadcast_to(gcc, (rows, LANES))

    row_seq = lax.broadcasted_iota(jnp.int32, (rows, GDN_DV), 0) >> _log2(seq_len)
    for n in range(GDN_SAMPLE_SEQS):
        seq = j * GDN_SAMPLE_SEQS + n
        mine = row_seq == seq
        last_row = seq * seq_len + (seq_len - 1)
        for r in range(GDN_REP):
            s_old = s0_ref[n, r]
            s16 = s_old.astype(BF16)
            v_new = u_ref[r] - _dot(w_ref[r], s16)
            vnew_ref[r] = jnp.where(mine, v_new, vnew_ref[r])
            qs_ref[r] = jnp.where(mine, _dot(qd_ref[r], s16), qs_ref[r])
            kd = jnp.where(mine, kd_ref[r].astype(F32), 0.0).astype(BF16)
            keep = jnp.exp(gcc_ref[r, pl.ds(last_row, 1), :])
            so_ref[n, r] = s_old * keep + _dot_tn(kd, v_new.astype(BF16))

    @pl.when(j == pl.num_programs(2) - 1)
    def _():
        for r in range(GDN_REP):
            cv = slice(r * GDN_DV, (r + 1) * GDN_DV)
            o = qs_ref[r] + _dot(attn_ref[r], vnew_ref[r].astype(BF16))
            o_ref[:, cv] = _gdn_gate_norm(o, z_ref[:, cv], nw_ref[...]).astype(o_ref.dtype)


def _gdn_sample(proj, ba, hist, prm, s0, seq_len):
    m = proj.shape[0]
    rows = SAMPLE_TILE
    nb = GDN_SAMPLE_SEQS
    steps = rows // seq_len // nb
    row_map = lambda i, kh, j: i
    zcol = (2 * GDN_KD + GDN_V) // (GDN_REP * GDN_DV)
    pair = GDN_REP * GDN_DV
    full = lambda shape: pl.BlockSpec(shape, lambda i, kh, j: (0,) * len(shape))
    state_spec = pl.BlockSpec((nb, GDN_REP, GDN_DK, GDN_DV), lambda i, kh, j: (i * steps + j, kh, 0, 0))
    head_f32 = pltpu.VMEM((GDN_REP, rows, GDN_DV), F32)
    head_bf16 = pltpu.VMEM((GDN_REP, rows, GDN_DV), BF16)
    return pl.pallas_call(
        functools.partial(_gdn_sample_kernel, seq_len=seq_len),
        grid=(m // rows, GDN_KH, steps),
        in_specs=_gdn_col_specs(rows, row_map) + [
            pl.BlockSpec((rows, pair), lambda i, kh, j: (i, zcol + kh)),
            pl.BlockSpec((rows, LANES), lambda i, kh, j: (i, 0)),
        ] + _gdn_col_specs(rows, row_map) + _gdn_col_specs(CONV_W, lambda i, kh, j: 0) + [
            full((1, LANES)), full((1, LANES)), full((1, GDN_DV)), state_spec],
        out_specs=[pl.BlockSpec((rows, pair), lambda i, kh, j: (i, kh)), state_spec],
        out_shape=[
            jax.ShapeDtypeStruct((m, GDN_V), BF16),
            jax.ShapeDtypeStruct(s0.shape, F32),
        ],
        scratch_shapes=[head_f32, head_bf16, head_bf16, head_bf16, head_bf16, head_f32, head_f32, head_f32],
        compiler_params=_params("parallel", "parallel", "arbitrary"),
        name="gdn_sample",
    )(proj, proj, proj, proj, ba, hist, hist, hist, prm["cw"], prm["cw"], prm["cw"], prm["dtb"], prm["alog"],
      prm["nw"], s0)


def _gdn_layer(x, s0, cbuf, norm_w, w_main, w_ba, prm, w_out, *, batch, seq, sample):
    proj = _rms_proj(x, norm_w, w_main, out_dtype=F32 if sample else BF16, tn=1024)
    ba = _rms_proj(x, norm_w, w_ba, out_dtype=F32, tn=LANES)
    conv_cols = proj[:, :2 * GDN_KD + GDN_V].reshape(batch, seq, -1)
    conv_new = conv_cols[:, seq - (CONV_W - 1):, :].astype(F32)
    if sample:
        o, s_new = _gdn_sample(proj, ba, _packed_history(cbuf, seq), prm, s0, seq)
    else:
        o, s_new = _gdn_prompt(proj, ba, prm, batch, seq)
    return _out_proj(o, w_out, x), s_new, conv_new


def _trunk(x, weights, states, *, batch, seq, pos0, sample):
    s_ret, s_ssd, c_ssd, s_gdn, c_gdn = states
    n_ret, n_ssd, n_ssdc, n_gdn, n_gdnc = [], [], [], [], []
    depth = weights["norm_mix"].shape[0]
    for i in range(depth):
        kind, jdx = i % 3, i // 3
        nm = weights["norm_mix"][i]
        if kind == 0:
            x, s = _ret_layer(x, None if s_ret is None else s_ret[jdx], nm, weights["ret_w_in"][jdx],
                              weights["ret_w_out"][jdx], batch=batch, seq=seq, pos0=pos0, sample=sample)
            n_ret.append(s)
        elif kind == 1:
            x, s, cnew = _ssd_layer(x, None if s_ssd is None else s_ssd[jdx], None if c_ssd is None else c_ssd[jdx],
                                    nm, weights["ssd_w_main"][jdx], weights["ssd_w_dt"][jdx], weights["ssd_prm"][jdx],
                                    weights["ssd_w_out"][jdx], batch=batch, seq=seq, sample=sample)
            n_ssd.append(s)
            n_ssdc.append(cnew)
        else:
            x, s, cnew = _gdn_layer(x, None if s_gdn is None else s_gdn[jdx], None if c_gdn is None else c_gdn[jdx],
                                    nm, weights["gdn_w_main"][jdx], weights["gdn_w_ba"][jdx], weights["gdn_prm"][jdx],
                                    weights["gdn_w_out"][jdx], batch=batch, seq=seq, sample=sample)
            n_gdn.append(s)
            n_gdnc.append(cnew)
        final_w = weights["norm_final"] if i == depth - 1 else None
        x = _mlp(x, weights["norm_mlp"][i], weights["mlp_w_up"][i], weights["mlp_w_down"][i], final_w)
    y = x.reshape(batch, seq, -1)
    return (y, jnp.stack(n_ret), jnp.stack(n_ssd), jnp.stack(n_ssdc), jnp.stack(n_gdn), jnp.stack(n_gdnc))


def _prepare_weights(norm_mix, norm_mlp, norm_final, ret_w_in, ret_w_out,
                     ssd_w_in, ssd_conv_w, ssd_conv_b, ssd_dt_bias, ssd_a_log, ssd_d, ssd_norm, ssd_w_out,
                     gdn_w_in, gdn_conv_w, gdn_dt_bias, gdn_a_log, gdn_norm, gdn_w_out, mlp_w_up, mlp_w_down):
    n_ssd, n_gdn = ssd_w_in.shape[0], gdn_w_in.shape[0]
    main = 3 * SSD_INNER

    def small(w):
        return jnp.pad(w[:, main:], ((0, 0), (0, LANES - (w.shape[1] - main)))).astype(BF16)

    ssd_prm = [dict(cw=ssd_conv_w[i], cb=ssd_conv_b[i].reshape(1, -1), dtb=_pad_lanes(ssd_dt_bias[i]),
                    alog=_pad_lanes(ssd_a_log[i]), dsk=jnp.repeat(ssd_d[i], SSD_HEADDIM).reshape(1, -1),
                    nw=ssd_norm[i].reshape(1, -1)) for i in range(n_ssd)]
    gdn_prm = [dict(cw=gdn_conv_w[i], dtb=_pad_lanes(gdn_dt_bias[i], GDN_VH), alog=_pad_lanes(gdn_a_log[i], GDN_VH),
                    nw=gdn_norm[i].reshape(1, -1)) for i in range(n_gdn)]
    return dict(
        norm_mix=norm_mix, norm_mlp=norm_mlp, norm_final=norm_final,
        ret_w_in=ret_w_in.astype(BF16), ret_w_out=ret_w_out.astype(BF16),
        ssd_w_main=[ssd_w_in[i, :, :main].astype(BF16) for i in range(n_ssd)],
        ssd_w_dt=[small(ssd_w_in[i]) for i in range(n_ssd)],
        ssd_prm=ssd_prm, ssd_w_out=ssd_w_out.astype(BF16),
        gdn_w_main=[gdn_w_in[i, :, :main].astype(BF16) for i in range(n_gdn)],
        gdn_w_ba=[small(gdn_w_in[i]) for i in range(n_gdn)],
        gdn_prm=gdn_prm, gdn_w_out=gdn_w_out.astype(BF16),
        mlp_w_up=mlp_w_up.astype(BF16), mlp_w_down=mlp_w_down.astype(BF16),
    )


def kernel(x_prompt, x_sample, state_ret, state_ssd, state_ssd_conv, state_gdn, state_gdn_conv, norm_mix, norm_mlp,
           norm_final, ret_w_in, ret_w_out, ssd_w_in, ssd_conv_w, ssd_conv_b, ssd_dt_bias, ssd_a_log, ssd_d, ssd_norm,
           ssd_w_out, gdn_w_in, gdn_conv_w, gdn_dt_bias, gdn_a_log, gdn_norm, gdn_w_out, mlp_w_up, mlp_w_down):
    weights = _prepare_weights(norm_mix, norm_mlp, norm_final, ret_w_in, ret_w_out, ssd_w_in, ssd_conv_w, ssd_conv_b,
                               ssd_dt_bias, ssd_a_log, ssd_d, ssd_norm, ssd_w_out, gdn_w_in, gdn_conv_w, gdn_dt_bias,
                               gdn_a_log, gdn_norm, gdn_w_out, mlp_w_up, mlp_w_down)
    bp, lp, d = x_prompt.shape
    bs, ls, _ = x_sample.shape
    assert lp % RET_CHUNK == 0 and lp % SSD_CHUNK == 0 and lp % GDN_CHUNK == 0
    assert SAMPLE_TILE % ls == 0 and (bs * ls) % SAMPLE_TILE == 0 and ls >= CONV_W - 1
    prompt = _trunk(x_prompt.reshape(bp * lp, d), weights, (None,) * 5, batch=bp, seq=lp, pos0=0, sample=False)
    sample = _trunk(x_sample.reshape(bs * ls, d), weights,
                    (state_ret, state_ssd, state_ssd_conv, state_gdn, state_gdn_conv),
                    batch=bs, seq=ls, pos0=PAST_LEN, sample=True)
    return (prompt[0], sample[0]) + prompt[1:] + sample[1:]
```

```python
import functools
import math

import jax
import jax.numpy as jnp
from jax import lax
from jax.experimental import pallas as pl
from jax.experimental.pallas import tpu as pltpu

F32 = jnp.float32
BF16 = jnp.bfloat16

D_MODEL = 1024
RET_HEADS = 4
RET_DK = 256
RET_DV = 512
RET_QK = RET_HEADS * RET_DK
RET_V = RET_HEADS * RET_DV
ROPE_BASE = 10000.0
SSD_INNER = 2048
SSD_HEADDIM = 64
SSD_HEADS = 32
SSD_GROUPS = 8
SSD_STATE = 128
SSD_GROUP_HEADS = SSD_HEADS // SSD_GROUPS
SSD_GROUP_INNER = SSD_INNER // SSD_GROUPS
GDN_KH = 8
GDN_VH = 16
GDN_DK = 128
GDN_DV = 128
GDN_KD = GDN_KH * GDN_DK
GDN_V = GDN_VH * GDN_DV
GDN_REP = GDN_VH // GDN_KH
CONV_W = 4
D_FF = 4096
RMS_EPS = 1e-6
PAST_LEN = 16384

LANES = 128
SUBLANES = 8
VMEM_LIMIT_BYTES = 56 * 1024 * 1024

SAMPLE_TILE = 128
RET_CHUNK = 256
SSD_CHUNK = 128
GDN_CHUNK = 128
HALO = SUBLANES

_NT = (((1,), (1,)), ((), ()))
_TN = (((0,), (0,)), ((), ()))


def _params(*sem):
    return pltpu.CompilerParams(dimension_semantics=sem, vmem_limit_bytes=VMEM_LIMIT_BYTES)


def _dot(a, b):
    return jnp.dot(a, b, preferred_element_type=F32)


def _dot_nt(a, b):
    return lax.dot_general(a, b, _NT, preferred_element_type=F32)


def _dot_tn(a, b):
    return lax.dot_general(a, b, _TN, preferred_element_type=F32)


def _rms_rows(x):
    return x * lax.rsqrt(jnp.mean(x * x, axis=-1, keepdims=True) + RMS_EPS)


def _log2(n):
    k = n.bit_length() - 1
    assert 1 << k == n, n
    return k


def _seq_masks(rows, seq_len):
    t = lax.broadcasted_iota(jnp.int32, (rows, rows), 0)
    s = lax.broadcasted_iota(jnp.int32, (rows, rows), 1)
    sh = _log2(seq_len)
    same = (t >> sh) == (s >> sh)
    return same & (t >= s), same & (t > s)


def _cumsum_rows(x, incl):
    hi = x.astype(BF16)
    r1 = x - hi.astype(F32)
    mid = r1.astype(BF16)
    lo = (r1 - mid.astype(F32)).astype(BF16)
    tri = jnp.where(incl, 1.0, 0.0).astype(BF16)
    parts = _dot(tri, jnp.concatenate([hi, mid, lo], axis=1))
    w = x.shape[1]
    return (parts[:, 2 * w:] + parts[:, w:2 * w]) + parts[:, :w]


def _proj_kernel(x_ref, nw_ref, w_ref, cos_ref, sin_ref, o_ref, h_ref, *, n_rot, k_scale):
    j = pl.program_id(1)

    @pl.when(j == 0)
    def _():
        h = _rms_rows(x_ref[...]) * nw_ref[...]
        h_ref[...] = h.astype(BF16)

    acc = _dot(h_ref[...], w_ref[...])
    if n_rot == 0:
        o_ref[...] = acc.astype(o_ref.dtype)
        return

    @pl.when(j < n_rot)
    def _():
        c, s = cos_ref[...], sin_ref[...]
        scale = jnp.where(j == 0, 1.0, k_scale).astype(F32)
        half = RET_DK // 2
        for hd in range(acc.shape[1] // RET_DK):
            lo = hd * RET_DK
            x1 = acc[:, lo:lo + half]
            x2 = acc[:, lo + half:lo + RET_DK]
            o_ref[:, lo:lo + half] = ((x1 * c - x2 * s) * scale).astype(o_ref.dtype)
            o_ref[:, lo + half:lo + RET_DK] = ((x1 * s + x2 * c) * scale).astype(o_ref.dtype)

    @pl.when(j >= n_rot)
    def _():
        o_ref[...] = acc.astype(o_ref.dtype)


def _rms_proj(x, norm_w, w_bf16, *, out_dtype, tn, rot=None):
    m, d = x.shape
    n = w_bf16.shape[1]
    tm = min(m, 1024)
    assert m % tm == 0 and n % tn == 0
    if rot is None:
        cos = sin = jnp.zeros((SUBLANES, LANES), F32)
        rot_spec = pl.BlockSpec((SUBLANES, LANES), lambda i, j: (0, 0))
        n_rot = 0
    else:
        cos, sin = rot
        period = cos.shape[0] // tm
        assert cos.shape[0] % tm == 0 and tn == RET_QK
        rot_spec = pl.BlockSpec((tm, LANES), lambda i, j: (i % period, 0))
        n_rot = 2
    kern = functools.partial(_proj_kernel, n_rot=n_rot, k_scale=RET_DK ** -0.5)
    return pl.pallas_call(
        kern,
        grid=(m // tm, n // tn),
        in_specs=[
            pl.BlockSpec((tm, d), lambda i, j: (i, 0)),
            pl.BlockSpec((1, d), lambda i, j: (0, 0)),
            pl.BlockSpec((d, tn), lambda i, j: (0, j)),
            rot_spec, rot_spec,
        ],
        out_specs=pl.BlockSpec((tm, tn), lambda i, j: (i, j)),
        out_shape=jax.ShapeDtypeStruct((m, n), out_dtype),
        scratch_shapes=[pltpu.VMEM((tm, d), BF16)],
        compiler_params=_params("parallel", "arbitrary"),
        name="rms_proj",
    )(x, norm_w.reshape(1, d), w_bf16, cos, sin)


def _out_proj_kernel(y_ref, w_ref, x_ref, o_ref):
    o_ref[...] = x_ref[...] + _dot(y_ref[...], w_ref[...])


def _out_proj(y, w_bf16, x):
    m, k = y.shape
    d = w_bf16.shape[1]
    tm = min(m, 512)
    return pl.pallas_call(
        _out_proj_kernel,
        grid=(m // tm,),
        in_specs=[
            pl.BlockSpec((tm, k), lambda i: (i, 0)),
            pl.BlockSpec((k, d), lambda i: (0, 0)),
            pl.BlockSpec((tm, d), lambda i: (i, 0)),
        ],
        out_specs=pl.BlockSpec((tm, d), lambda i: (i, 0)),
        out_shape=jax.ShapeDtypeStruct((m, d), F32),
        compiler_params=_params("parallel"),
        name="out_proj",
    )(y, w_bf16, x)


def _mlp_kernel(x_ref, nw_ref, wu_ref, wd_ref, fw_ref, o_ref, h_ref, acc_ref, *, final):
    j = pl.program_id(1)

    @pl.when(j == 0)
    def _():
        h_ref[...] = (_rms_rows(x_ref[...]) * nw_ref[...]).astype(BF16)
        acc_ref[...] = jnp.zeros(acc_ref.shape, F32)

    u = _dot(h_ref[...], wu_ref[...])
    a = jnp.square(jnp.maximum(u, 0.0)).astype(BF16)
    acc_ref[...] += _dot(a, wd_ref[...])

    @pl.when(j == pl.num_programs(1) - 1)
    def _():
        y = x_ref[...] + acc_ref[...]
        if final:
            y = _rms_rows(y) * fw_ref[...]
        o_ref[...] = y


def _mlp(x, norm_w, wu_bf16, wd_bf16, final_w=None):
    m, d = x.shape
    ff = wu_bf16.shape[1]
    tm = min(m, 1024)
    tf = 1024
    final = final_w is not None
    fw = (final_w if final else jnp.zeros((d,), F32)).reshape(1, d)
    return pl.pallas_call(
        functools.partial(_mlp_kernel, final=final),
        grid=(m // tm, ff // tf),
        in_specs=[
            pl.BlockSpec((tm, d), lambda i, j: (i, 0)),
            pl.BlockSpec((1, d), lambda i, j: (0, 0)),
            pl.BlockSpec((d, tf), lambda i, j: (0, j)),
            pl.BlockSpec((tf, d), lambda i, j: (j, 0)),
            pl.BlockSpec((1, d), lambda i, j: (0, 0)),
        ],
        out_specs=pl.BlockSpec((tm, d), lambda i, j: (i, 0)),
        out_shape=jax.ShapeDtypeStruct((m, d), F32),
        scratch_shapes=[pltpu.VMEM((tm, d), BF16), pltpu.VMEM((tm, d), F32)],
        compiler_params=_params("parallel", "arbitrary"),
        name="mlp",
    )(x, norm_w.reshape(1, d), wu_bf16, wd_bf16, fw)


def _ret_log_gamma(h):
    return math.log1p(-(2.0 ** (-5.0 - h)))


def _ret_decay(rows, seq_len, h):
    lg = _ret_log_gamma(h)
    t = lax.broadcasted_iota(jnp.int32, (rows, rows), 0)
    s = lax.broadcasted_iota(jnp.int32, (rows, rows), 1)
    incl, _ = _seq_masks(rows, seq_len)
    d = jnp.where(incl, jnp.exp(jnp.where(incl, (t - s).astype(F32) * lg, 0.0)), 0.0)
    pos = (lax.broadcasted_iota(jnp.int32, (rows, 1), 0) & (seq_len - 1)).astype(F32)
    return d, pos, lg


def _ret_gate_norm(y, g):
    return jax.nn.silu(g.astype(F32)) * _rms_rows(y)


def _ret_prompt_kernel(q_ref, k_ref, v_ref, g_ref, y_ref, so_ref, s_ref):
    c = pl.program_id(1)
    rows = q_ref.shape[0]

    @pl.when(c == 0)
    def _():
        s_ref[...] = jnp.zeros(s_ref.shape, F32)

    for h in range(RET_HEADS):
        d, pos, lg = _ret_decay(rows, rows, h)
        q = q_ref[:, h * RET_DK:(h + 1) * RET_DK]
        k = k_ref[:, h * RET_DK:(h + 1) * RET_DK]
        v = v_ref[:, h * RET_DV:(h + 1) * RET_DV]
        a = (_dot_nt(q, k) * d).astype(BF16)
        s_old = s_ref[h]
        y = _dot(a, v) + jnp.exp((pos + 1.0) * lg) * _dot(q, s_old.astype(BF16))
        kw = (k.astype(F32) * jnp.exp((rows - 1.0 - pos) * lg)).astype(BF16)
        s_ref[h] = s_old * math.exp(rows * lg) + _dot_tn(kw, v)
        y_ref[:, h * RET_DV:(h + 1) * RET_DV] = _ret_gate_norm(
            y, g_ref[:, h * RET_DV:(h + 1) * RET_DV]).astype(y_ref.dtype)

    @pl.when(c == pl.num_programs(1) - 1)
    def _():
        so_ref[0] = s_ref[...]


def _without_arg(fn, idx):
    def wrapped(*refs, **kw):
        return fn(*refs[:idx], *refs[idx + 1:], **kw)
    return wrapped


def _stacked_state_out(n_layers, layer, batch, carried, n_in, index_of):
    spec = pl.BlockSpec((None, 1, RET_HEADS, RET_DK, RET_DV), lambda *ix: (layer, index_of(*ix), 0, 0, 0))
    shape = jax.ShapeDtypeStruct((n_layers, batch, RET_HEADS, RET_DK, RET_DV), F32)
    extra_in = [] if carried is None else [pl.BlockSpec(memory_space=pl.ANY)]
    extra_args = [] if carried is None else [carried]
    aliases = {} if carried is None else {n_in: 1}
    return spec, shape, extra_in, extra_args, aliases


def _ret_prompt(proj, batch, seq, layer, n_layers, carried):
    t = min(RET_CHUNK, seq)
    nc = seq // t
    row = lambda b, c: b * nc + c
    n_in = 4
    s_spec, s_shape, extra_in, extra_args, aliases = _stacked_state_out(
        n_layers, layer, batch, carried, n_in, lambda b, c: b)
    kern = _ret_prompt_kernel if carried is None else _without_arg(_ret_prompt_kernel, n_in)
    return pl.pallas_call(
        kern,
        grid=(batch, nc),
        in_specs=[
            pl.BlockSpec((t, RET_QK), lambda b, c: (row(b, c), 0)),
            pl.BlockSpec((t, RET_QK), lambda b, c: (row(b, c), 1)),
            pl.BlockSpec((t, RET_V), lambda b, c: (row(b, c), 1)),
            pl.BlockSpec((t, RET_V), lambda b, c: (row(b, c), 2)),
        ] + extra_in,
        out_specs=[pl.BlockSpec((t, RET_V), lambda b, c: (row(b, c), 0)), s_spec],
        out_shape=[jax.ShapeDtypeStruct((batch * seq, RET_V), BF16), s_shape],
        scratch_shapes=[pltpu.VMEM((RET_HEADS, RET_DK, RET_DV), F32)],
        input_output_aliases=aliases,
        compiler_params=_params("parallel", "arbitrary"),
        name="ret_prompt",
    )(proj, proj, proj, proj, *extra_args)


def _ret_sample_kernel(q_ref, k_ref, v_ref, g_ref, s0_ref, y_ref, so_ref, acc_ref, *, seq_len):
    j = pl.program_id(1)
    rows = q_ref.shape[0]
    row_seq = lax.broadcasted_iota(jnp.int32, (rows, 1), 0) >> _log2(seq_len)
    mine = row_seq == j

    for h in range(RET_HEADS):
        d, pos, lg = _ret_decay(rows, seq_len, h)
        q = q_ref[:, h * RET_DK:(h + 1) * RET_DK].astype(BF16)
        k = k_ref[:, h * RET_DK:(h + 1) * RET_DK]
        v = v_ref[:, h * RET_DV:(h + 1) * RET_DV].astype(BF16)
        cols = slice(h * RET_DV, (h + 1) * RET_DV)

        @pl.when(j == 0)
        def _():
            a = (_dot_nt(q, k.astype(BF16)) * d).astype(BF16)
            acc_ref[:, cols] = _dot(a, v)

        s_old = s0_ref[0, h]
        y_state = jnp.exp((pos + 1.0) * lg) * _dot(q, s_old.astype(BF16))
        acc_ref[:, cols] += jnp.where(mine, y_state, 0.0)
        kw = jnp.where(mine, k.astype(F32) * jnp.exp((seq_len - 1.0 - pos) * lg), 0.0).astype(BF16)
        so_ref[0, h] = s_old * math.exp(seq_len * lg) + _dot_tn(kw, v)

    @pl.when(j == pl.num_programs(1) - 1)
    def _():
        for h in range(RET_HEADS):
            cols = slice(h * RET_DV, (h + 1) * RET_DV)
            y_ref[:, cols] = _ret_gate_norm(acc_ref[:, cols], g_ref[:, cols]).astype(y_ref.dtype)


def _ret_sample(proj, states, seq_len, layer, carried):
    m = proj.shape[0]
    rows = SAMPLE_TILE
    per_tile = rows // seq_len
    n_layers, batch = states.shape[:2]
    seq_of = lambda i, j: i * per_tile + j
    n_in = 5
    s_spec, s_shape, extra_in, extra_args, aliases = _stacked_state_out(
        n_layers, layer, batch, carried, n_in, seq_of)
    kern = functools.partial(_ret_sample_kernel, seq_len=seq_len)
    if carried is not None:
        kern = _without_arg(kern, n_in)
    return pl.pallas_call(
        kern,
        grid=(m // rows, per_tile),
        in_specs=[
            pl.BlockSpec((rows, RET_QK), lambda i, j: (i, 0)),
            pl.BlockSpec((rows, RET_QK), lambda i, j: (i, 1)),
            pl.BlockSpec((rows, RET_V), lambda i, j: (i, 1)),
            pl.BlockSpec((rows, RET_V), lambda i, j: (i, 2)),
            pl.BlockSpec((None, 1, RET_HEADS, RET_DK, RET_DV), lambda i, j: (layer, seq_of(i, j), 0, 0, 0)),
        ] + extra_in,
        out_specs=[pl.BlockSpec((rows, RET_V), lambda i, j: (i, 0)), s_spec],
        out_shape=[jax.ShapeDtypeStruct((m, RET_V), BF16), s_shape],
        scratch_shapes=[pltpu.VMEM((rows, RET_V), F32)],
        input_output_aliases=aliases,
        compiler_params=_params("parallel", "arbitrary"),
        name="ret_sample",
    )(proj, proj, proj, proj, states, *extra_args)


def _rope_tables(seq, pos0, reps):
    half = RET_DK // 2
    inv = ROPE_BASE ** (-jnp.arange(half, dtype=F32) / half)
    ang = (jnp.arange(seq, dtype=F32) + pos0)[:, None] * inv[None, :]
    return jnp.tile(jnp.cos(ang), (reps, 1)), jnp.tile(jnp.sin(ang), (reps, 1))


def _ret_layer(x, states, carried, layer, n_layers, norm_w, w_in, w_out, *, batch, seq, pos0, sample):
    m = x.shape[0]
    if sample:
        rot = _rope_tables(seq, pos0, min(m, 1024) // seq)
    else:
        rot = _rope_tables(seq, pos0, 1)
    proj = _rms_proj(x, norm_w, w_in, out_dtype=F32 if sample else BF16, tn=RET_QK, rot=rot)
    if sample:
        y, stack = _ret_sample(proj, states, seq, layer, carried)
    else:
        y, stack = _ret_prompt(proj, batch, seq, layer, n_layers, carried)
    return _out_proj(y, w_out, x), stack


def _conv_chunk(x, prev, cw_ref, cols):
    def taps(src, lo, hi):
        acc = None
        for tap in range(CONV_W):
            back = CONV_W - 1 - tap
            cur = (src if back == 0 else pltpu.roll(src, back, 0))[lo:hi] * cw_ref[tap:tap + 1, cols]
            acc = cur if acc is None else acc + cur
        return acc

    rows = x.shape[0]
    head = taps(jnp.concatenate([prev, x[0:HALO]], axis=0), HALO, 2 * HALO)
    return jnp.concatenate([head, taps(x, HALO, rows)], axis=0)


def _last_rows(x_ref):
    rows = x_ref.shape[0]
    return x_ref[rows - 2 * HALO:rows, :].astype(F32)[HALO:2 * HALO]


def _conv_packed(u, hist, cw_ref, cols, seq_len):
    rows = u.shape[0]
    pos = lax.broadcasted_iota(jnp.int32, (rows, 1), 0) & (seq_len - 1)
    acc = None
    for tap in range(CONV_W):
        back = CONV_W - 1 - tap
        cur = u if back == 0 else pltpu.roll(u, back, 0)
        if tap < CONV_W - 1:
            old = hist if tap == 0 else pltpu.roll(hist, rows - tap, 0)
            cur = jnp.where(pos + tap <= CONV_W - 2, old, cur)
        term = cur * cw_ref[tap:tap + 1, cols]
        acc = term if acc is None else acc + term
    return acc


def _last_of_seq(x, seq_len):
    rows = x.shape[0]
    if seq_len == rows:
        return jnp.broadcast_to(x[rows - 1:rows, :], x.shape)
    pos = lax.broadcasted_iota(jnp.int32, x.shape, 0) & (seq_len - 1)
    out = x
    for ahead in range(1, seq_len):
        out = jnp.where(pos == seq_len - 1 - ahead, pltpu.roll(x, rows - ahead, 0), out)
    return out


def _pad_lanes(v, offset=0):
    return jnp.pad(v.astype(F32), (offset, LANES - offset - v.shape[0])).reshape(1, LANES)


def _packed_history(buf, seq_len):
    b, r, c = buf.shape
    return jnp.pad(buf, ((0, 0), (0, seq_len - r), (0, 0))).reshape(b * seq_len, c)


def _ssd_expand(x, g):
    rows = x.shape[0]
    head = lax.broadcasted_iota(jnp.int32, (rows, SSD_GROUP_INNER), 1) >> _log2(SSD_HEADDIM)
    h0 = g * SSD_GROUP_HEADS
    out = jnp.broadcast_to(x[:, h0:h0 + 1], (rows, SSD_GROUP_INNER))
    for r in range(1, SSD_GROUP_HEADS):
        out = jnp.where(head == r, jnp.broadcast_to(x[:, h0 + r:h0 + r + 1], (rows, SSD_GROUP_INNER)), out)
    return out


def _ssd_step_terms(dt_ref, dtb_ref, alog_ref, seq_len):
    dt = jax.nn.softplus(dt_ref[...] + dtb_ref[...])
    la = dt * (-jnp.exp(alog_ref[...]))
    ct = _cumsum_rows(la, _seq_masks(la.shape[0], seq_len)[0])
    return dt, ct


def _ssd_group_intra(g, xs, bg, cg, dt, ct, ct_t, incl):
    rows = xs.shape[0]
    scores = _dot_nt(cg, bg)
    xdt = xs * _ssd_expand(dt, g)
    head = lax.broadcasted_iota(jnp.int32, (rows, SSD_GROUP_INNER), 1) >> _log2(SSD_HEADDIM)
    y = jnp.zeros((rows, SSD_GROUP_INNER), F32)
    for r in range(SSD_GROUP_HEADS):
        h = g * SSD_GROUP_HEADS + r
        diff = ct[:, h:h + 1] - ct_t[h:h + 1, :]
        d = jnp.where(incl, jnp.exp(jnp.where(incl, diff, 0.0)), 0.0)
        a = (scores * d).astype(BF16)
        y = y + _dot(a, jnp.where(head == r, xdt, 0.0).astype(BF16))
    return y, xdt, _ssd_expand(ct, g)


def _ssd_gate_norm(y, z, nw):
    return _rms_rows(y * jax.nn.silu(z.astype(F32))) * nw


def _ssd_prompt_kernel(z_ref, x_ref, bc_ref, dt_ref, cwx_ref, cwbc_ref, cbx_ref, cbbc_ref, dtb_ref, alog_ref,
                       dsk_ref, nw_ref, y_ref, so_ref, px_ref, pbc_ref, st_ref):
    c = pl.program_id(1)
    rows = x_ref.shape[0]
    gs = SSD_GROUPS * SSD_STATE

    @pl.when(c == 0)
    def _():
        px_ref[...] = jnp.zeros(px_ref.shape, F32)
        pbc_ref[...] = jnp.zeros(pbc_ref.shape, F32)
        st_ref[...] = jnp.zeros(st_ref.shape, F32)

    dt, ct = _ssd_step_terms(dt_ref, dtb_ref, alog_ref, rows)
    ct_t = ct.T
    incl, _ = _seq_masks(rows, rows)

    for g in range(SSD_GROUPS):
        ci = slice(g * SSD_GROUP_INNER, (g + 1) * SSD_GROUP_INNER)
        cb_ = slice(g * SSD_STATE, (g + 1) * SSD_STATE)
        cc_ = slice(gs + g * SSD_STATE, gs + (g + 1) * SSD_STATE)
        xs = jax.nn.silu(_conv_chunk(x_ref[:, ci].astype(F32), px_ref[:, ci], cwx_ref, ci) + cbx_ref[:, ci])
        bg = jax.nn.silu(_conv_chunk(bc_ref[:, cb_].astype(F32), pbc_ref[:, cb_], cwbc_ref, cb_)
                         + cbbc_ref[:, cb_]).astype(BF16)
        cg = jax.nn.silu(_conv_chunk(bc_ref[:, cc_].astype(F32), pbc_ref[:, cc_], cwbc_ref, cc_)
                         + cbbc_ref[:, cc_]).astype(BF16)
        y, xdt, ctc = _ssd_group_intra(g, xs, bg, cg, dt, ct, ct_t, incl)
        st = st_ref[g]
        y = y + jnp.exp(ctc) * _dot(cg, st.astype(BF16)) + dsk_ref[:, ci] * xs
        last = ctc[rows - 1:rows, :]
        st_ref[g] = st * jnp.exp(last) + _dot_tn(bg, (xdt * jnp.exp(last - ctc)).astype(BF16))
        y_ref[:, ci] = _ssd_gate_norm(y, z_ref[:, ci], nw_ref[:, ci]).astype(y_ref.dtype)

    px_ref[...] = _last_rows(x_ref)
    pbc_ref[...] = _last_rows(bc_ref)

    @pl.when(c == pl.num_programs(1) - 1)
    def _():
        for g in range(SSD_GROUPS):
            hs = slice(g * SSD_GROUP_HEADS, (g + 1) * SSD_GROUP_HEADS)
            so_ref[0, hs] = st_ref[g].T.reshape(SSD_GROUP_HEADS, SSD_HEADDIM, SSD_STATE)


def _ssd_prompt(proj, dtp, prm, batch, seq):
    t = min(SSD_CHUNK, seq)
    nc = seq // t
    row = lambda b, c: b * nc + c
    half = SSD_INNER
    full = lambda shape: pl.BlockSpec(shape, lambda b, c: (0,) * len(shape))
    return pl.pallas_call(
        _ssd_prompt_kernel,
        grid=(batch, nc),
        in_specs=[
            pl.BlockSpec((t, half), lambda b, c: (row(b, c), 0)),
            pl.BlockSpec((t, half), lambda b, c: (row(b, c), 1)),
            pl.BlockSpec((t, half), lambda b, c: (row(b, c), 2)),
            pl.BlockSpec((t, LANES), lambda b, c: (row(b, c), 0)),
            pl.BlockSpec((CONV_W, half), lambda b, c: (0, 0)),
            pl.BlockSpec((CONV_W, half), lambda b, c: (0, 1)),
            pl.BlockSpec((1, half), lambda b, c: (0, 0)),
            pl.BlockSpec((1, half), lambda b, c: (0, 1)),
            full((1, LANES)), full((1, LANES)), full((1, half)), full((1, half)),
        ],
        out_specs=[
            pl.BlockSpec((t, half), lambda b, c: (row(b, c), 0)),
            pl.BlockSpec((1, SSD_HEADS, SSD_HEADDIM, SSD_STATE), lambda b, c: (b, 0, 0, 0)),
        ],
        out_shape=[
            jax.ShapeDtypeStruct((batch * seq, half), BF16),
            jax.ShapeDtypeStruct((batch, SSD_HEADS, SSD_HEADDIM, SSD_STATE), F32),
        ],
        scratch_shapes=[
            pltpu.VMEM((HALO, half), F32),
            pltpu.VMEM((HALO, half), F32),
            pltpu.VMEM((SSD_GROUPS, SSD_STATE, SSD_GROUP_INNER), F32),
        ],
        compiler_params=_params("parallel", "arbitrary"),
        name="ssd_prompt",
    )(proj, proj, proj, dtp, prm["cw"], prm["cw"], prm["cb"], prm["cb"], prm["dtb"], prm["alog"],
      prm["dsk"], prm["nw"])


SSD_SAMPLE_SEQS = 4


def _ssd_sample_kernel(z_ref, x_ref, bc_ref, hx_ref, hbc_ref, dt_ref, cwx_ref, cwbc_ref, cbx_ref, cbbc_ref,
                       dtb_ref, alog_ref, dsk_ref, nw_ref, s0_ref, y_ref, so_ref,
                       acc_ref, ect_ref, xdw_ref, b_ref, c_ref, ct_ref, *, seq_len):
    j = pl.program_id(1)
    rows = x_ref.shape[0]
    gs = SSD_GROUPS * SSD_STATE

    @pl.when(j == 0)
    def _():
        dt, ct = _ssd_step_terms(dt_ref, dtb_ref, alog_ref, seq_len)
        ct_ref[...] = ct
        ct_t = ct.T
        incl, _ = _seq_masks(rows, seq_len)
        for g in range(SSD_GROUPS):
            ci = slice(g * SSD_GROUP_INNER, (g + 1) * SSD_GROUP_INNER)
            cb_ = slice(g * SSD_STATE, (g + 1) * SSD_STATE)
            cc_ = slice(gs + g * SSD_STATE, gs + (g + 1) * SSD_STATE)
            xs = jax.nn.silu(_conv_packed(x_ref[:, ci], hx_ref[:, ci], cwx_ref, ci, seq_len) + cbx_ref[:, ci])
            bg = jax.nn.silu(_conv_packed(bc_ref[:, cb_], hbc_ref[:, cb_], cwbc_ref, cb_, seq_len)
                             + cbbc_ref[:, cb_]).astype(BF16)
            cg = jax.nn.silu(_conv_packed(bc_ref[:, cc_], hbc_ref[:, cc_], cwbc_ref, cc_, seq_len)
                             + cbbc_ref[:, cc_]).astype(BF16)
            y, xdt, ctc = _ssd_group_intra(g, xs, bg, cg, dt, ct, ct_t, incl)
            acc_ref[:, ci] = y + dsk_ref[:, ci] * xs
            ect_ref[:, ci] = jnp.exp(ctc)
            xdw_ref[:, ci] = xdt * jnp.exp(_last_of_seq(ctc, seq_len) - ctc)
            b_ref[:, cb_] = bg
            c_ref[:, cb_] = cg

    row_seq = lax.broadcasted_iota(jnp.int32, (rows, 1), 0) >> _log2(seq_len)
    for n in range(SSD_SAMPLE_SEQS):
        seq = j * SSD_SAMPLE_SEQS + n
        mine = row_seq == seq
        last_row = seq * seq_len + (seq_len - 1)
        for g in range(SSD_GROUPS):
            ci = slice(g * SSD_GROUP_INNER, (g + 1) * SSD_GROUP_INNER)
            cb_ = slice(g * SSD_STATE, (g + 1) * SSD_STATE)
            hs = slice(g * SSD_GROUP_HEADS, (g + 1) * SSD_GROUP_HEADS)
            sg = s0_ref[n, hs].reshape(SSD_GROUP_INNER, SSD_STATE)
            y_state = ect_ref[:, ci] * _dot_nt(c_ref[:, cb_], sg.astype(BF16))
            acc_ref[:, ci] += jnp.where(mine, y_state, 0.0)
            upd = _dot_tn(jnp.where(mine, xdw_ref[:, ci], 0.0).astype(BF16), b_ref[:, cb_])
            for r in range(SSD_GROUP_HEADS):
                h = g * SSD_GROUP_HEADS + r
                keep = jnp.exp(ct_ref[pl.ds(last_row, 1), pl.ds(h, 1)])
                so_ref[n, h] = s0_ref[n, h] * keep + upd[r * SSD_HEADDIM:(r + 1) * SSD_HEADDIM, :]

    @pl.when(j == pl.num_programs(1) - 1)
    def _():
        for g in range(SSD_GROUPS):
            ci = slice(g * SSD_GROUP_INNER, (g + 1) * SSD_GROUP_INNER)
            y_ref[:, ci] = _ssd_gate_norm(acc_ref[:, ci], z_ref[:, ci], nw_ref[:, ci]).astype(y_ref.dtype)


def _ssd_sample(proj, dtp, hist, prm, s0, seq_len):
    m = proj.shape[0]
    rows = SAMPLE_TILE
    nb = SSD_SAMPLE_SEQS
    steps = rows // seq_len // nb
    half = SSD_INNER
    full = lambda shape: pl.BlockSpec(shape, lambda i, j: (0,) * len(shape))
    state_spec = pl.BlockSpec((nb, SSD_HEADS, SSD_HEADDIM, SSD_STATE), lambda i, j: (i * steps + j, 0, 0, 0))
    return pl.pallas_call(
        functools.partial(_ssd_sample_kernel, seq_len=seq_len),
        grid=(m // rows, steps),
        in_specs=[
            pl.BlockSpec((rows, half), lambda i, j: (i, 0)),
            pl.BlockSpec((rows, half), lambda i, j: (i, 1)),
            pl.BlockSpec((rows, half), lambda i, j: (i, 2)),
            pl.BlockSpec((rows, half), lambda i, j: (i, 0)),
            pl.BlockSpec((rows, half), lambda i, j: (i, 1)),
            pl.BlockSpec((rows, LANES), lambda i, j: (i, 0)),
            pl.BlockSpec((CONV_W, half), lambda i, j: (0, 0)),
            pl.BlockSpec((CONV_W, half), lambda i, j: (0, 1)),
            pl.BlockSpec((1, half), lambda i, j: (0, 0)),
            pl.BlockSpec((1, half), lambda i, j: (0, 1)),
            full((1, LANES)), full((1, LANES)), full((1, half)), full((1, half)),
            state_spec,
        ],
        out_specs=[pl.BlockSpec((rows, half), lambda i, j: (i, 0)), state_spec],
        out_shape=[
            jax.ShapeDtypeStruct((m, half), BF16),
            jax.ShapeDtypeStruct(s0.shape, F32),
        ],
        scratch_shapes=[
            pltpu.VMEM((rows, half), F32),
            pltpu.VMEM((rows, half), F32),
            pltpu.VMEM((rows, half), F32),
            pltpu.VMEM((rows, SSD_GROUPS * SSD_STATE), BF16),
            pltpu.VMEM((rows, SSD_GROUPS * SSD_STATE), BF16),
            pltpu.VMEM((rows, LANES), F32),
        ],
        compiler_params=_params("parallel", "arbitrary"),
        name="ssd_sample",
    )(proj, proj, proj, hist, hist, dtp, prm["cw"], prm["cw"], prm["cb"], prm["cb"], prm["dtb"], prm["alog"],
      prm["dsk"], prm["nw"], s0)


def _ssd_layer(x, s0, cbuf, norm_w, w_main, w_dt, prm, w_out, *, batch, seq, sample):
    proj = _rms_proj(x, norm_w, w_main, out_dtype=F32 if sample else BF16, tn=1024)
    dtp = _rms_proj(x, norm_w, w_dt, out_dtype=F32, tn=LANES)
    conv_cols = proj[:, SSD_INNER:].reshape(batch, seq, -1)
    conv_new = conv_cols[:, seq - (CONV_W - 1):, :].astype(F32)
    if sample:
        y, s_new = _ssd_sample(proj, dtp, _packed_history(cbuf, seq), prm, s0, seq)
    else:
        y, s_new = _ssd_prompt(proj, dtp, prm, batch, seq)
    return _out_proj(y, w_out, x), s_new, conv_new


def _l2norm_rows(x):
    return x * lax.rsqrt(jnp.sum(x * x, axis=-1, keepdims=True) + RMS_EPS)


def _gdn_gates(ba_ref, dtb_ref, alog_ref, seq_len):
    ba = ba_ref[...]
    beta = jax.nn.sigmoid(ba)
    g = -jnp.exp(alog_ref[...]) * jax.nn.softplus(ba + dtb_ref[...])
    gc = _cumsum_rows(g, _seq_masks(g.shape[0], seq_len)[0])
    return beta, gc, _last_of_seq(gc, seq_len)


INV_BASE_BLOCK = 16
GDN_PROMPT_KEY_HEADS = 4


def _unit_lower_inverse(ls, seq_len):
    rows = ls[0].shape[0]
    ti = lax.broadcasted_iota(jnp.int32, (rows, rows), 0)
    si = lax.broadcasted_iota(jnp.int32, (rows, rows), 1)
    base = min(INV_BASE_BLOCK, seq_len)
    diag_block = (ti >> _log2(base)) == (si >> _log2(base))
    ns = [jnp.where(diag_block, -l, 0.0) for l in ls]
    eye = jnp.where(ti == si, 1.0, 0.0)
    xs = [eye + n for n in ns]
    for _ in range(_log2(base) - 1):
        n16s = [n.astype(BF16) for n in ns]
        ns = [_dot(n16, n16) for n16 in n16s]
        xs = [x + _dot(x.astype(BF16), n.astype(BF16)) for x, n in zip(xs, ns)]
    blk = base
    while blk < seq_len:
        lower_left = ((ti >> _log2(2 * blk)) == (si >> _log2(2 * blk))) & ((ti >> _log2(blk)) != (si >> _log2(blk)))
        cs = [jnp.where(lower_left, l, 0.0).astype(BF16) for l in ls]
        x16s = [x.astype(BF16) for x in xs]
        ts = [_dot(x16, c).astype(BF16) for x16, c in zip(x16s, cs)]
        xs = [x - _dot(t, x16) for x, t, x16 in zip(xs, ts, x16s)]
        blk *= 2
    return xs


def _gdn_heads_intra(qs, ks, kks, qks, vs, beta_all, gc_all, gl_all, gc_t, heads, incl, strict, seq_len):
    rows = qs[0].shape[0]
    betas, gccs, gcls, ds = [], [], [], []
    for head in heads:
        if isinstance(head, int):
            beta = beta_all[:, head:head + 1]
            gcc = gc_all[:, GDN_VH + head:GDN_VH + head + 1]
            gcl = gl_all[:, GDN_VH + head:GDN_VH + head + 1]
            gcr = gc_t[GDN_VH + head:GDN_VH + head + 1, :]
        else:
            lane = lax.broadcasted_iota(jnp.int32, (rows, LANES), 1)
            sub = lax.broadcasted_iota(jnp.int32, (LANES, rows), 0)
            beta = jnp.sum(jnp.where(lane == head, beta_all, 0.0), axis=1, keepdims=True)
            gcc = jnp.sum(jnp.where(lane == GDN_VH + head, gc_all, 0.0), axis=1, keepdims=True)
            gcl = jnp.sum(jnp.where(lane == GDN_VH + head, gl_all, 0.0), axis=1, keepdims=True)
            gcr = jnp.sum(jnp.where(sub == GDN_VH + head, gc_t, 0.0), axis=0, keepdims=True)
        betas.append(beta)
        gccs.append(gcc)
        gcls.append(gcl)
        ds.append(jnp.where(incl, jnp.exp(jnp.where(incl, gcc - gcr, 0.0)), 0.0))
    key = lambda i: i // GDN_REP
    n = len(heads)
    ps = _unit_lower_inverse(
        [jnp.where(strict, (betas[i] * kks[key(i)]) * ds[i], 0.0) for i in range(n)], seq_len)
    egcs = [jnp.exp(g) for g in gccs]
    rhss = [jnp.concatenate([vs[i] * betas[i], ks[key(i)] * (betas[i] * egcs[i])], axis=1).astype(BF16)
            for i in range(n)]
    sols = [_dot(ps[i].astype(BF16), rhss[i]) for i in range(n)]
    us = [s[:, :GDN_DV] for s in sols]
    ws = [s[:, GDN_DV:].astype(BF16) for s in sols]
    attns = [(qks[key(i)] * ds[i]).astype(BF16) for i in range(n)]
    qds = [(qs[key(i)] * egcs[i]).astype(BF16) for i in range(n)]
    kds = [(ks[key(i)] * jnp.exp(gcls[i] - gccs[i])).astype(BF16) for i in range(n)]
    return us, ws, attns, qds, kds, gccs


def _gdn_gate_norm(o, z, nw):
    return _rms_rows(o) * nw * jax.nn.silu(z.astype(F32))


def _gdn_prompt_kernel(q_ref, k_ref, v_ref, z_ref, ba_ref, cwq_ref, cwk_ref, cwv_ref, dtb_ref, alog_ref, nw_ref,
                       o_ref, so_ref, pq_ref, pk_ref, pv_ref, s_ref):
    c = pl.program_id(1)
    rows = q_ref.shape[0]

    @pl.when(c == 0)
    def _():
        for prev in (pq_ref, pk_ref, pv_ref):
            prev[...] = jnp.zeros(prev.shape, F32)
        s_ref[...] = jnp.zeros(s_ref.shape, F32)

    beta_all, gc_all, gl_all = _gdn_gates(ba_ref, dtb_ref, alog_ref, rows)
    gc_t = gc_all.T
    incl, strict = _seq_masks(rows, rows)

    for kh0 in range(0, GDN_KH, GDN_PROMPT_KEY_HEADS):
        khs = range(kh0, kh0 + GDN_PROMPT_KEY_HEADS)
        heads = [kh * GDN_REP + r for kh in khs for r in range(GDN_REP)]
        key_cols = [slice(kh * GDN_DK, (kh + 1) * GDN_DK) for kh in khs]
        val_cols = [slice(h * GDN_DV, (h + 1) * GDN_DV) for h in heads]
        qs = [_l2norm_rows(jax.nn.silu(_conv_chunk(q_ref[:, c_].astype(F32), pq_ref[:, c_], cwq_ref, c_)))
              * (GDN_DK ** -0.5) for c_ in key_cols]
        ks = [_l2norm_rows(jax.nn.silu(_conv_chunk(k_ref[:, c_].astype(F32), pk_ref[:, c_], cwk_ref, c_)))
              for c_ in key_cols]
        vs = [jax.nn.silu(_conv_chunk(v_ref[:, c_].astype(F32), pv_ref[:, c_], cwv_ref, c_)) for c_ in val_cols]
        q16s = [q.astype(BF16) for q in qs]
        k16s = [k.astype(BF16) for k in ks]
        kks = [_dot_nt(k16, k16) for k16 in k16s]
        qks = [_dot_nt(q16, k16) for q16, k16 in zip(q16s, k16s)]
        us, ws, attns, qds, kds, gccs = _gdn_heads_intra(
            qs, ks, kks, qks, vs, beta_all, gc_all, gl_all, gc_t, heads, incl, strict, rows)
        n = len(heads)
        s_olds = [s_ref[h] for h in heads]
        s16s = [s.astype(BF16) for s in s_olds]
        v_news = [us[i] - _dot(ws[i], s16s[i]) for i in range(n)]
        vn16s = [v.astype(BF16) for v in v_news]
        os_ = [_dot(qds[i], s16s[i]) + _dot(attns[i], vn16s[i]) for i in range(n)]
        for i, h in enumerate(heads):
            s_ref[h] = s_olds[i] * jnp.exp(gccs[i][rows - 1:rows, :]) + _dot_tn(kds[i], vn16s[i])
            o_ref[:, val_cols[i]] = _gdn_gate_norm(os_[i], z_ref[:, val_cols[i]], nw_ref[...]).astype(o_ref.dtype)

    pq_ref[...] = _last_rows(q_ref)
    pk_ref[...] = _last_rows(k_ref)
    pv_ref[...] = _last_rows(v_ref)

    @pl.when(c == pl.num_programs(1) - 1)
    def _():
        so_ref[0] = s_ref[...]


def _gdn_col_specs(rows, row_map):
    kcol = GDN_KD // GDN_DK
    vcol = 2 * GDN_KD // (GDN_REP * GDN_DV)
    return [
        pl.BlockSpec((rows, GDN_DK), lambda *ix: (row_map(*ix), ix[1])),
        pl.BlockSpec((rows, GDN_DK), lambda *ix: (row_map(*ix), kcol + ix[1])),
        pl.BlockSpec((rows, GDN_REP * GDN_DV), lambda *ix: (row_map(*ix), vcol + ix[1])),
    ]


def _gdn_prompt(proj, ba, prm, batch, seq):
    t = min(GDN_CHUNK, seq)
    nc = seq // t
    row = lambda b, c: b * nc + c
    full = lambda shape: pl.BlockSpec(shape, lambda b, c: (0,) * len(shape))
    return pl.pallas_call(
        _gdn_prompt_kernel,
        grid=(batch, nc),
        in_specs=[
            pl.BlockSpec((t, GDN_KD), lambda b, c: (row(b, c), 0)),
            pl.BlockSpec((t, GDN_KD), lambda b, c: (row(b, c), 1)),
            pl.BlockSpec((t, GDN_V), lambda b, c: (row(b, c), 1)),
            pl.BlockSpec((t, GDN_V), lambda b, c: (row(b, c), 2)),
            pl.BlockSpec((t, LANES), lambda b, c: (row(b, c), 0)),
            pl.BlockSpec((CONV_W, GDN_KD), lambda b, c: (0, 0)),
            pl.BlockSpec((CONV_W, GDN_KD), lambda b, c: (0, 1)),
            pl.BlockSpec((CONV_W, GDN_V), lambda b, c: (0, 1)),
            full((1, LANES)), full((1, LANES)), full((1, GDN_DV)),
        ],
        out_specs=[
            pl.BlockSpec((t, GDN_V), lambda b, c: (row(b, c), 0)),
            pl.BlockSpec((1, GDN_VH, GDN_DK, GDN_DV), lambda b, c: (b, 0, 0, 0)),
        ],
        out_shape=[
            jax.ShapeDtypeStruct((batch * seq, GDN_V), BF16),
            jax.ShapeDtypeStruct((batch, GDN_VH, GDN_DK, GDN_DV), F32),
        ],
        scratch_shapes=[
            pltpu.VMEM((HALO, GDN_KD), F32),
            pltpu.VMEM((HALO, GDN_KD), F32),
            pltpu.VMEM((HALO, GDN_V), F32),
            pltpu.VMEM((GDN_VH, GDN_DK, GDN_DV), F32),
        ],
        compiler_params=_params("parallel", "arbitrary"),
        name="gdn_prompt",
    )(proj, proj, proj, proj, ba, prm["cw"], prm["cw"], prm["cw"], prm["dtb"], prm["alog"], prm["nw"])


GDN_SAMPLE_SEQS = 8


def _gdn_sample_kernel(q_ref, k_ref, v_ref, z_ref, ba_ref, hq_ref, hk_ref, hv_ref, cwq_ref, cwk_ref, cwv_ref,
                       dtb_ref, alog_ref, nw_ref, s0_ref, o_ref, so_ref,
                       u_ref, w_ref, attn_ref, qd_ref, kd_ref, gcc_ref, vnew_ref, qs_ref, *, seq_len):
    kh = pl.program_id(1)
    j = pl.program_id(2)
    rows = q_ref.shape[0]

    @pl.when(j == 0)
    def _():
        one = slice(0, GDN_DK)
        q = _l2norm_rows(jax.nn.silu(_conv_packed(q_ref[...], hq_ref[...], cwq_ref, one, seq_len))) * (GDN_DK ** -0.5)
        k = _l2norm_rows(jax.nn.silu(_conv_packed(k_ref[...], hk_ref[...], cwk_ref, one, seq_len)))
        v_all = jax.nn.silu(_conv_packed(v_ref[...], hv_ref[...], cwv_ref, slice(0, GDN_REP * GDN_DV), seq_len))
        beta_all, gc_all, gl_all = _gdn_gates(ba_ref, dtb_ref, alog_ref, seq_len)
        gc_t = gc_all.T
        incl, strict = _seq_masks(rows, seq_len)
        q16, k16 = q.astype(BF16), k.astype(BF16)
        kk = _dot_nt(k16, k16)
        qk = _dot_nt(q16, k16)
        vnew_ref[...] = jnp.zeros(vnew_ref.shape, F32)
        qs_ref[...] = jnp.zeros(qs_ref.shape, F32)
        vs = [v_all[:, r * GDN_DV:(r + 1) * GDN_DV] for r in range(GDN_REP)]
        heads = [kh * GDN_REP + r for r in range(GDN_REP)]
        us, ws, attns, qds, kds, gccs = _gdn_heads_intra(
            [q], [k], [kk], [qk], vs, beta_all, gc_all, gl_all, gc_t, heads, incl, strict, seq_len)
        for r in range(GDN_REP):
            u_ref[r] = us[r]
            w_ref[r] = ws[r]
            attn_ref[r] = attns[r]
            qd_ref[r] = qds[r]
            kd_ref[r] = kds[r]
            gcc_ref[r] = jnp.broadcast_to(gccs[r], (rows, LANES))

    row_seq = lax.broadcasted_iota(jnp.int32, (rows, GDN_DV), 0) >> _log2(seq_len)
    for n in range(GDN_SAMPLE_SEQS):
        seq = j * GDN_SAMPLE_SEQS + n
        mine = row_seq == seq
        last_row = seq * seq_len + (seq_len - 1)
        for r in range(GDN_REP):
            s_old = s0_ref[n, r]
            s16 = s_old.astype(BF16)
            v_new = u_ref[r] - _dot(w_ref[r], s16)
            vnew_ref[r] = jnp.where(mine, v_new, vnew_ref[r])
            qs_ref[r] = jnp.where(mine, _dot(qd_ref[r], s16), qs_ref[r])
            kd = jnp.where(mine, kd_ref[r].astype(F32), 0.0).astype(BF16)
            keep = jnp.exp(gcc_ref[r, pl.ds(last_row, 1), :])
            so_ref[n, r] = s_old * keep + _dot_tn(kd, v_new.astype(BF16))

    @pl.when(j == pl.num_programs(2) - 1)
    def _():
        for r in range(GDN_REP):
            cv = slice(r * GDN_DV, (r + 1) * GDN_DV)
            o = qs_ref[r] + _dot(attn_ref[r], vnew_ref[r].astype(BF16))
            o_ref[:, cv] = _gdn_gate_norm(o, z_ref[:, cv], nw_ref[...]).astype(o_ref.dtype)


def _gdn_sample(proj, ba, hist, prm, s0, seq_len):
    m = proj.shape[0]
    rows = SAMPLE_TILE
    nb = GDN_SAMPLE_SEQS
    steps = rows // seq_len // nb
    row_map = lambda i, kh, j: i
    zcol = (2 * GDN_KD + GDN_V) // (GDN_REP * GDN_DV)
    pair = GDN_REP * GDN_DV
    full = lambda shape: pl.BlockSpec(shape, lambda i, kh, j: (0,) * len(shape))
    state_spec = pl.BlockSpec((nb, GDN_REP, GDN_DK, GDN_DV), lambda i, kh, j: (i * steps + j, kh, 0, 0))
    head_f32 = pltpu.VMEM((GDN_REP, rows, GDN_DV), F32)
    head_bf16 = pltpu.VMEM((GDN_REP, rows, GDN_DV), BF16)
    return pl.pallas_call(
        functools.partial(_gdn_sample_kernel, seq_len=seq_len),
        grid=(m // rows, GDN_KH, steps),
        in_specs=_gdn_col_specs(rows, row_map) + [
            pl.BlockSpec((rows, pair), lambda i, kh, j: (i, zcol + kh)),
            pl.BlockSpec((rows, LANES), lambda i, kh, j: (i, 0)),
        ] + _gdn_col_specs(rows, row_map) + _gdn_col_specs(CONV_W, lambda i, kh, j: 0) + [
            full((1, LANES)), full((1, LANES)), full((1, GDN_DV)), state_spec],
        out_specs=[pl.BlockSpec((rows, pair), lambda i, kh, j: (i, kh)), state_spec],
        out_shape=[
            jax.ShapeDtypeStruct((m, GDN_V), BF16),
            jax.ShapeDtypeStruct(s0.shape, F32),
        ],
        scratch_shapes=[head_f32, head_bf16, head_bf16, head_bf16, head_bf16, head_f32, head_f32, head_f32],
        compiler_params=_params("parallel", "parallel", "arbitrary"),
        name="gdn_sample",
    )(proj, proj, proj, proj, ba, hist, hist, hist, prm["cw"], prm["cw"], prm["cw"], prm["dtb"], prm["alog"],
      prm["nw"], s0)


def _gdn_layer(x, s0, cbuf, norm_w, w_main, w_ba, prm, w_out, *, batch, seq, sample):
    proj = _rms_proj(x, norm_w, w_main, out_dtype=F32 if sample else BF16, tn=1024)
    ba = _rms_proj(x, norm_w, w_ba, out_dtype=F32, tn=LANES)
    conv_cols = proj[:, :2 * GDN_KD + GDN_V].reshape(batch, seq, -1)
    conv_new = conv_cols[:, seq - (CONV_W - 1):, :].astype(F32)
    if sample:
        o, s_new = _gdn_sample(proj, ba, _packed_history(cbuf, seq), prm, s0, seq)
    else:
        o, s_new = _gdn_prompt(proj, ba, prm, batch, seq)
    return _out_proj(o, w_out, x), s_new, conv_new


def _trunk(x, weights, states, *, batch, seq, pos0, sample):
    s_ret, s_ssd, c_ssd, s_gdn, c_gdn = states
    n_ssd, n_ssdc, n_gdn, n_gdnc = [], [], [], []
    ret_stack = None
    depth = weights["norm_mix"].shape[0]
    n_ret_layers = weights["ret_w_in"].shape[0]
    for i in range(depth):
        kind, jdx = i % 3, i // 3
        nm = weights["norm_mix"][i]
        if kind == 0:
            x, ret_stack = _ret_layer(x, s_ret, ret_stack, jdx, n_ret_layers, nm, weights["ret_w_in"][jdx],
                                      weights["ret_w_out"][jdx], batch=batch, seq=seq, pos0=pos0, sample=sample)
        elif kind == 1:
            x, s, cnew = _ssd_layer(x, None if s_ssd is None else s_ssd[jdx], None if c_ssd is None else c_ssd[jdx],
                                    nm, weights["ssd_w_main"][jdx], weights["ssd_w_dt"][jdx], weights["ssd_prm"][jdx],
                                    weights["ssd_w_out"][jdx], batch=batch, seq=seq, sample=sample)
            n_ssd.append(s)
            n_ssdc.append(cnew)
        else:
            x, s, cnew = _gdn_layer(x, None if s_gdn is None else s_gdn[jdx], None if c_gdn is None else c_gdn[jdx],
                                    nm, weights["gdn_w_main"][jdx], weights["gdn_w_ba"][jdx], weights["gdn_prm"][jdx],
                                    weights["gdn_w_out"][jdx], batch=batch, seq=seq, sample=sample)
            n_gdn.append(s)
            n_gdnc.append(cnew)
        final_w = weights["norm_final"] if i == depth - 1 else None
        x = _mlp(x, weights["norm_mlp"][i], weights["mlp_w_up"][i], weights["mlp_w_down"][i], final_w)
    y = x.reshape(batch, seq, -1)
    return (y, ret_stack, jnp.stack(n_ssd), jnp.stack(n_ssdc), jnp.stack(n_gdn), jnp.stack(n_gdnc))


def _prepare_weights(norm_mix, norm_mlp, norm_final, ret_w_in, ret_w_out,
                     ssd_w_in, ssd_conv_w, ssd_conv_b, ssd_dt_bias, ssd_a_log, ssd_d, ssd_norm, ssd_w_out,
                     gdn_w_in, gdn_conv_w, gdn_dt_bias, gdn_a_log, gdn_norm, gdn_w_out, mlp_w_up, mlp_w_down):
    n_ssd, n_gdn = ssd_w_in.shape[0], gdn_w_in.shape[0]
    main = 3 * SSD_INNER

    def small(w):
        return jnp.pad(w[:, main:], ((0, 0), (0, LANES - (w.shape[1] - main)))).astype(BF16)

    ssd_prm = [dict(cw=ssd_conv_w[i], cb=ssd_conv_b[i].reshape(1, -1), dtb=_pad_lanes(ssd_dt_bias[i]),
                    alog=_pad_lanes(ssd_a_log[i]), dsk=jnp.repeat(ssd_d[i], SSD_HEADDIM).reshape(1, -1),
                    nw=ssd_norm[i].reshape(1, -1)) for i in range(n_ssd)]
    gdn_prm = [dict(cw=gdn_conv_w[i], dtb=_pad_lanes(gdn_dt_bias[i], GDN_VH), alog=_pad_lanes(gdn_a_log[i], GDN_VH),
                    nw=gdn_norm[i].reshape(1, -1)) for i in range(n_gdn)]
    return dict(
        norm_mix=norm_mix, norm_mlp=norm_mlp, norm_final=norm_final,
        ret_w_in=ret_w_in.astype(BF16), ret_w_out=ret_w_out.astype(BF16),
        ssd_w_main=[ssd_w_in[i, :, :main].astype(BF16) for i in range(n_ssd)],
        ssd_w_dt=[small(ssd_w_in[i]) for i in range(n_ssd)],
        ssd_prm=ssd_prm, ssd_w_out=ssd_w_out.astype(BF16),
        gdn_w_main=[gdn_w_in[i, :, :main].astype(BF16) for i in range(n_gdn)],
        gdn_w_ba=[small(gdn_w_in[i]) for i in range(n_gdn)],
        gdn_prm=gdn_prm, gdn_w_out=gdn_w_out.astype(BF16),
        mlp_w_up=mlp_w_up.astype(BF16), mlp_w_down=mlp_w_down.astype(BF16),
    )


def kernel(x_prompt, x_sample, state_ret, state_ssd, state_ssd_conv, state_gdn, state_gdn_conv, norm_mix, norm_mlp,
           norm_final, ret_w_in, ret_w_out, ssd_w_in, ssd_conv_w, ssd_conv_b, ssd_dt_bias, ssd_a_log, ssd_d, ssd_norm,
           ssd_w_out, gdn_w_in, gdn_conv_w, gdn_dt_bias, gdn_a_log, gdn_norm, gdn_w_out, mlp_w_up, mlp_w_down):
    weights = _prepare_weights(norm_mix, norm_mlp, norm_final, ret_w_in, ret_w_out, ssd_w_in, ssd_conv_w, ssd_conv_b,
                               ssd_dt_bias, ssd_a_log, ssd_d, ssd_norm, ssd_w_out, gdn_w_in, gdn_conv_w, gdn_dt_bias,
                               gdn_a_log, gdn_norm, gdn_w_out, mlp_w_up, mlp_w_down)
    bp, lp, d = x_prompt.shape
    bs, ls, _ = x_sample.shape
    assert lp % RET_CHUNK == 0 and lp % SSD_CHUNK == 0 and lp % GDN_CHUNK == 0
    assert SAMPLE_TILE % ls == 0 and (bs * ls) % SAMPLE_TILE == 0 and ls >= CONV_W - 1
    prompt = _trunk(x_prompt.reshape(bp * lp, d), weights, (None,) * 5, batch=bp, seq=lp, pos0=0, sample=False)
    sample = _trunk(x_sample.reshape(bs * ls, d), weights,
                    (state_ret, state_ssd, state_ssd_conv, state_gdn, state_gdn_conv),
                    batch=bs, seq=ls, pos0=PAST_LEN, sample=True)
    return (prompt[0], sample[0]) + prompt[1:] + sample[1:]
```

```python
import functools
import math

import jax
import jax.numpy as jnp
from jax import lax
from jax.experimental import pallas as pl
from jax.experimental.pallas import tpu as pltpu

F32 = jnp.float32
BF16 = jnp.bfloat16

D_MODEL = 1024
RET_HEADS = 4
RET_DK = 256
RET_DV = 512
RET_QK = RET_HEADS * RET_DK
RET_V = RET_HEADS * RET_DV
ROPE_BASE = 10000.0
SSD_INNER = 2048
SSD_HEADDIM = 64
SSD_HEADS = 32
SSD_GROUPS = 8
SSD_STATE = 128
SSD_GROUP_HEADS = SSD_HEADS // SSD_GROUPS
SSD_GROUP_INNER = SSD_INNER // SSD_GROUPS
GDN_KH = 8
GDN_VH = 16
GDN_DK = 128
GDN_DV = 128
GDN_KD = GDN_KH * GDN_DK
GDN_V = GDN_VH * GDN_DV
GDN_REP = GDN_VH // GDN_KH
CONV_W = 4
D_FF = 4096
RMS_EPS = 1e-6
PAST_LEN = 16384

LANES = 128
SUBLANES = 8
VMEM_LIMIT_BYTES = 56 * 1024 * 1024

SAMPLE_TILE = 128
RET_CHUNK = 256
SSD_CHUNK = 128
GDN_CHUNK = 128
HALO = SUBLANES

_NT = (((1,), (1,)), ((), ()))
_TN = (((0,), (0,)), ((), ()))


def _params(*sem):
    return pltpu.CompilerParams(dimension_semantics=sem, vmem_limit_bytes=VMEM_LIMIT_BYTES)


def _dot(a, b):
    return jnp.dot(a, b, preferred_element_type=F32)


def _dot_nt(a, b):
    return lax.dot_general(a, b, _NT, preferred_element_type=F32)


def _dot_tn(a, b):
    return lax.dot_general(a, b, _TN, preferred_element_type=F32)


def _rms_rows(x):
    return x * lax.rsqrt(jnp.mean(x * x, axis=-1, keepdims=True) + RMS_EPS)


def _log2(n):
    k = n.bit_length() - 1
    assert 1 << k == n, n
    return k


def _seq_masks(rows, seq_len):
    t = lax.broadcasted_iota(jnp.int32, (rows, rows), 0)
    s = lax.broadcasted_iota(jnp.int32, (rows, rows), 1)
    sh = _log2(seq_len)
    same = (t >> sh) == (s >> sh)
    return same & (t >= s), same & (t > s)


def _cumsum_rows(x, incl):
    hi = x.astype(BF16)
    r1 = x - hi.astype(F32)
    mid = r1.astype(BF16)
    lo = (r1 - mid.astype(F32)).astype(BF16)
    tri = jnp.where(incl, 1.0, 0.0).astype(BF16)
    parts = _dot(tri, jnp.concatenate([hi, mid, lo], axis=1))
    w = x.shape[1]
    return (parts[:, 2 * w:] + parts[:, w:2 * w]) + parts[:, :w]


DENSE_ROWS = 512


def _resident(shape):
    return pl.BlockSpec(shape, lambda i: (0,) * len(shape), pipeline_mode=pl.Buffered(1))


def _proj_kernel(x_ref, nw_ref, w_ref, cos_ref, sin_ref, o_ref, *, tn, n_rot, k_scale):
    h = (_rms_rows(x_ref[...]) * nw_ref[...]).astype(BF16)
    for j in range(w_ref.shape[1] // tn):
        cols = slice(j * tn, (j + 1) * tn)
        acc = _dot(h, w_ref[:, cols])
        if j >= n_rot:
            o_ref[:, cols] = acc.astype(o_ref.dtype)
            continue
        c, s = cos_ref[...], sin_ref[...]
        scale = 1.0 if j == 0 else k_scale
        half = RET_DK // 2
        for hd in range(tn // RET_DK):
            lo = j * tn + hd * RET_DK
            x1 = acc[:, hd * RET_DK:hd * RET_DK + half]
            x2 = acc[:, hd * RET_DK + half:(hd + 1) * RET_DK]
            o_ref[:, lo:lo + half] = ((x1 * c - x2 * s) * scale).astype(o_ref.dtype)
            o_ref[:, lo + half:lo + RET_DK] = ((x1 * s + x2 * c) * scale).astype(o_ref.dtype)


def _rms_proj(x, norm_w, w_bf16, *, out_dtype, tn, rot=None):
    m, d = x.shape
    n = w_bf16.shape[1]
    tm = min(m, DENSE_ROWS)
    assert m % tm == 0 and n % tn == 0
    if rot is None:
        cos = sin = jnp.zeros((SUBLANES, LANES), F32)
        rot_spec = pl.BlockSpec((SUBLANES, LANES), lambda i: (0, 0))
        n_rot = 0
    else:
        cos, sin = rot
        period = cos.shape[0] // tm
        assert cos.shape[0] % tm == 0 and tn == RET_QK
        rot_spec = pl.BlockSpec((tm, LANES), lambda i: (i % period, 0))
        n_rot = 2
    kern = functools.partial(_proj_kernel, tn=tn, n_rot=n_rot, k_scale=RET_DK ** -0.5)
    return pl.pallas_call(
        kern,
        grid=(m // tm,),
        in_specs=[
            pl.BlockSpec((tm, d), lambda i: (i, 0)),
            _resident((1, d)),
            _resident((d, n)),
            rot_spec, rot_spec,
        ],
        out_specs=pl.BlockSpec((tm, n), lambda i: (i, 0)),
        out_shape=jax.ShapeDtypeStruct((m, n), out_dtype),
        compiler_params=_params("parallel"),
        name="rms_proj",
    )(x, norm_w.reshape(1, d), w_bf16, cos, sin)


MLP_FF_TILE = 1024


def _mix_mlp_kernel(y_ref, wo_ref, x_ref, nw_ref, wu_ref, wd_ref, fw_ref, o_ref, *, final):
    x1 = x_ref[...] + _dot(y_ref[...], wo_ref[...])
    h = (_rms_rows(x1) * nw_ref[...]).astype(BF16)
    acc = x1
    for f in range(wu_ref.shape[1] // MLP_FF_TILE):
        cols = slice(f * MLP_FF_TILE, (f + 1) * MLP_FF_TILE)
        u = _dot(h, wu_ref[:, cols])
        acc = acc + _dot(jnp.square(jnp.maximum(u, 0.0)).astype(BF16), wd_ref[cols, :])
    if final:
        acc = _rms_rows(acc) * fw_ref[...]
    o_ref[...] = acc


def _mix_mlp(y, wo_bf16, x, norm_w, wu_bf16, wd_bf16, final_w=None):
    m, d = x.shape
    k = y.shape[1]
    ff = wu_bf16.shape[1]
    tm = min(m, DENSE_ROWS)
    final = final_w is not None
    fw = (final_w if final else jnp.zeros((d,), F32)).reshape(1, d)
    return pl.pallas_call(
        functools.partial(_mix_mlp_kernel, final=final),
        grid=(m // tm,),
        in_specs=[
            pl.BlockSpec((tm, k), lambda i: (i, 0)),
            _resident((k, d)),
            pl.BlockSpec((tm, d), lambda i: (i, 0)),
            _resident((1, d)),
            _resident((d, ff)),
            _resident((ff, d)),
            _resident((1, d)),
        ],
        out_specs=pl.BlockSpec((tm, d), lambda i: (i, 0)),
        out_shape=jax.ShapeDtypeStruct((m, d), F32),
        compiler_params=_params("parallel"),
        name="mix_mlp",
    )(y, wo_bf16, x, norm_w.reshape(1, d), wu_bf16, wd_bf16, fw)


def _ret_log_gamma(h):
    return math.log1p(-(2.0 ** (-5.0 - h)))


def _ret_decay(rows, seq_len, h):
    lg = _ret_log_gamma(h)
    t = lax.broadcasted_iota(jnp.int32, (rows, rows), 0)
    s = lax.broadcasted_iota(jnp.int32, (rows, rows), 1)
    incl, _ = _seq_masks(rows, seq_len)
    d = jnp.where(incl, jnp.exp(jnp.where(incl, (t - s).astype(F32) * lg, 0.0)), 0.0)
    pos = (lax.broadcasted_iota(jnp.int32, (rows, 1), 0) & (seq_len - 1)).astype(F32)
    return d, pos, lg


def _ret_gate_norm(y, g):
    return jax.nn.silu(g.astype(F32)) * _rms_rows(y)


def _ret_prompt_kernel(q_ref, k_ref, v_ref, g_ref, y_ref, so_ref, s_ref):
    c = pl.program_id(1)
    rows = q_ref.shape[0]

    @pl.when(c == 0)
    def _():
        s_ref[...] = jnp.zeros(s_ref.shape, F32)

    for h in range(RET_HEADS):
        d, pos, lg = _ret_decay(rows, rows, h)
        q = q_ref[:, h * RET_DK:(h + 1) * RET_DK]
        k = k_ref[:, h * RET_DK:(h + 1) * RET_DK]
        v = v_ref[:, h * RET_DV:(h + 1) * RET_DV]
        a = (_dot_nt(q, k) * d).astype(BF16)
        s_old = s_ref[h]
        y = _dot(a, v) + jnp.exp((pos + 1.0) * lg) * _dot(q, s_old.astype(BF16))
        kw = (k.astype(F32) * jnp.exp((rows - 1.0 - pos) * lg)).astype(BF16)
        s_ref[h] = s_old * math.exp(rows * lg) + _dot_tn(kw, v)
        y_ref[:, h * RET_DV:(h + 1) * RET_DV] = _ret_gate_norm(
            y, g_ref[:, h * RET_DV:(h + 1) * RET_DV]).astype(y_ref.dtype)

    @pl.when(c == pl.num_programs(1) - 1)
    def _():
        so_ref[0] = s_ref[...]


def _without_arg(fn, idx):
    def wrapped(*refs, **kw):
        return fn(*refs[:idx], *refs[idx + 1:], **kw)
    return wrapped


def _stacked_state_out(n_layers, layer, batch, carried, n_in, index_of, block=(1, RET_HEADS, RET_DK, RET_DV)):
    spec = pl.BlockSpec((None,) + block, lambda *ix: (layer, index_of(*ix), 0, 0, 0))
    shape = jax.ShapeDtypeStruct((n_layers, batch) + block[1:], F32)
    extra_in = [] if carried is None else [pl.BlockSpec(memory_space=pl.ANY)]
    extra_args = [] if carried is None else [carried]
    aliases = {} if carried is None else {n_in: 1}
    return spec, shape, extra_in, extra_args, aliases


def _ret_prompt(proj, batch, seq, layer, n_layers, carried):
    t = min(RET_CHUNK, seq)
    nc = seq // t
    row = lambda b, c: b * nc + c
    n_in = 4
    s_spec, s_shape, extra_in, extra_args, aliases = _stacked_state_out(
        n_layers, layer, batch, carried, n_in, lambda b, c: b)
    kern = _ret_prompt_kernel if carried is None else _without_arg(_ret_prompt_kernel, n_in)
    return pl.pallas_call(
        kern,
        grid=(batch, nc),
        in_specs=[
            pl.BlockSpec((t, RET_QK), lambda b, c: (row(b, c), 0)),
            pl.BlockSpec((t, RET_QK), lambda b, c: (row(b, c), 1)),
            pl.BlockSpec((t, RET_V), lambda b, c: (row(b, c), 1)),
            pl.BlockSpec((t, RET_V), lambda b, c: (row(b, c), 2)),
        ] + extra_in,
        out_specs=[pl.BlockSpec((t, RET_V), lambda b, c: (row(b, c), 0)), s_spec],
        out_shape=[jax.ShapeDtypeStruct((batch * seq, RET_V), BF16), s_shape],
        scratch_shapes=[pltpu.VMEM((RET_HEADS, RET_DK, RET_DV), F32)],
        input_output_aliases=aliases,
        compiler_params=_params("parallel", "arbitrary"),
        name="ret_prompt",
    )(proj, proj, proj, proj, *extra_args)


def _ret_sample_kernel(q_ref, k_ref, v_ref, g_ref, s0_ref, y_ref, so_ref, acc_ref, *, seq_len):
    j = pl.program_id(1)
    rows = q_ref.shape[0]
    row_seq = lax.broadcasted_iota(jnp.int32, (rows, 1), 0) >> _log2(seq_len)
    mine = row_seq == j

    for h in range(RET_HEADS):
        d, pos, lg = _ret_decay(rows, seq_len, h)
        q = q_ref[:, h * RET_DK:(h + 1) * RET_DK].astype(BF16)
        k = k_ref[:, h * RET_DK:(h + 1) * RET_DK]
        v = v_ref[:, h * RET_DV:(h + 1) * RET_DV].astype(BF16)
        cols = slice(h * RET_DV, (h + 1) * RET_DV)

        @pl.when(j == 0)
        def _():
            a = (_dot_nt(q, k.astype(BF16)) * d).astype(BF16)
            acc_ref[:, cols] = _dot(a, v)

        s_old = s0_ref[0, h]
        y_state = jnp.exp((pos + 1.0) * lg) * _dot(q, s_old.astype(BF16))
        acc_ref[:, cols] += jnp.where(mine, y_state, 0.0)
        kw = jnp.where(mine, k.astype(F32) * jnp.exp((seq_len - 1.0 - pos) * lg), 0.0).astype(BF16)
        so_ref[0, h] = s_old * math.exp(seq_len * lg) + _dot_tn(kw, v)

    @pl.when(j == pl.num_programs(1) - 1)
    def _():
        for h in range(RET_HEADS):
            cols = slice(h * RET_DV, (h + 1) * RET_DV)
            y_ref[:, cols] = _ret_gate_norm(acc_ref[:, cols], g_ref[:, cols]).astype(y_ref.dtype)


def _ret_sample(proj, states, seq_len, layer, carried):
    m = proj.shape[0]
    rows = SAMPLE_TILE
    per_tile = rows // seq_len
    n_layers, batch = states.shape[:2]
    seq_of = lambda i, j: i * per_tile + j
    n_in = 5
    s_spec, s_shape, extra_in, extra_args, aliases = _stacked_state_out(
        n_layers, layer, batch, carried, n_in, seq_of)
    kern = functools.partial(_ret_sample_kernel, seq_len=seq_len)
    if carried is not None:
        kern = _without_arg(kern, n_in)
    return pl.pallas_call(
        kern,
        grid=(m // rows, per_tile),
        in_specs=[
            pl.BlockSpec((rows, RET_QK), lambda i, j: (i, 0)),
            pl.BlockSpec((rows, RET_QK), lambda i, j: (i, 1)),
            pl.BlockSpec((rows, RET_V), lambda i, j: (i, 1)),
            pl.BlockSpec((rows, RET_V), lambda i, j: (i, 2)),
            pl.BlockSpec((None, 1, RET_HEADS, RET_DK, RET_DV), lambda i, j: (layer, seq_of(i, j), 0, 0, 0)),
        ] + extra_in,
        out_specs=[pl.BlockSpec((rows, RET_V), lambda i, j: (i, 0)), s_spec],
        out_shape=[jax.ShapeDtypeStruct((m, RET_V), BF16), s_shape],
        scratch_shapes=[pltpu.VMEM((rows, RET_V), F32)],
        input_output_aliases=aliases,
        compiler_params=_params("parallel", "arbitrary"),
        name="ret_sample",
    )(proj, proj, proj, proj, states, *extra_args)


def _rope_tables(seq, pos0, reps):
    half = RET_DK // 2
    inv = ROPE_BASE ** (-jnp.arange(half, dtype=F32) / half)
    ang = (jnp.arange(seq, dtype=F32) + pos0)[:, None] * inv[None, :]
    return jnp.tile(jnp.cos(ang), (reps, 1)), jnp.tile(jnp.sin(ang), (reps, 1))


def _ret_layer(x, states, carried, layer, n_layers, norm_w, w_in, *, batch, seq, pos0, sample):
    m = x.shape[0]
    if sample:
        rot = _rope_tables(seq, pos0, min(m, DENSE_ROWS) // seq)
    else:
        rot = _rope_tables(seq, pos0, 1)
    proj = _rms_proj(x, norm_w, w_in, out_dtype=F32 if sample else BF16, tn=RET_QK, rot=rot)
    if sample:
        y, stack = _ret_sample(proj, states, seq, layer, carried)
    else:
        y, stack = _ret_prompt(proj, batch, seq, layer, n_layers, carried)
    return y, stack


def _conv_chunk(x, prev, cw_ref, cols):
    def taps(src, lo, hi):
        acc = None
        for tap in range(CONV_W):
            back = CONV_W - 1 - tap
            cur = (src if back == 0 else pltpu.roll(src, back, 0))[lo:hi] * cw_ref[tap:tap + 1, cols]
            acc = cur if acc is None else acc + cur
        return acc

    rows = x.shape[0]
    head = taps(jnp.concatenate([prev, x[0:HALO]], axis=0), HALO, 2 * HALO)
    return jnp.concatenate([head, taps(x, HALO, rows)], axis=0)


def _conv_chunk_mxu(x16, prev, cw_ref):
    rows = x16.shape[0]
    ti = lax.broadcasted_iota(jnp.int32, (rows, rows), 0)
    si = lax.broadcasted_iota(jnp.int32, (rows, rows), 1)
    shifts = jnp.concatenate(
        [jnp.where(ti - si == CONV_W - 1 - tap, 1.0, 0.0) for tap in range(CONV_W - 1)], axis=0).astype(BF16)
    moved = _dot(shifts, x16)
    x = x16.astype(F32)
    acc = moved[0:rows] * cw_ref[0:1, :]
    for tap in range(1, CONV_W - 1):
        acc = acc + moved[tap * rows:(tap + 1) * rows] * cw_ref[tap:tap + 1, :]
    acc = acc + x * cw_ref[CONV_W - 1:CONV_W, :]
    window = jnp.concatenate([prev, x[0:HALO]], axis=0)
    head = None
    for tap in range(CONV_W):
        back = CONV_W - 1 - tap
        cur = (window if back == 0 else pltpu.roll(window, back, 0))[HALO:2 * HALO] * cw_ref[tap:tap + 1, :]
        head = cur if head is None else head + cur
    return jnp.concatenate([head, acc[HALO:rows]], axis=0)


def _last_rows(x_ref):
    rows = x_ref.shape[0]
    return x_ref[rows - 2 * HALO:rows, :].astype(F32)[HALO:2 * HALO]


def _conv_packed(u, hist, cw_ref, cols, seq_len):
    rows = u.shape[0]
    pos = lax.broadcasted_iota(jnp.int32, (rows, 1), 0) & (seq_len - 1)
    acc = None
    for tap in range(CONV_W):
        back = CONV_W - 1 - tap
        cur = u if back == 0 else pltpu.roll(u, back, 0)
        if tap < CONV_W - 1:
            old = hist if tap == 0 else pltpu.roll(hist, rows - tap, 0)
            cur = jnp.where(pos + tap <= CONV_W - 2, old, cur)
        term = cur * cw_ref[tap:tap + 1, cols]
        acc = term if acc is None else acc + term
    return acc


def _last_of_seq(x, seq_len):
    rows = x.shape[0]
    if seq_len == rows:
        return jnp.broadcast_to(x[rows - 1:rows, :], x.shape)
    pos = lax.broadcasted_iota(jnp.int32, x.shape, 0) & (seq_len - 1)
    out = x
    for ahead in range(1, seq_len):
        out = jnp.where(pos == seq_len - 1 - ahead, pltpu.roll(x, rows - ahead, 0), out)
    return out


def _pad_lanes(v, offset=0):
    return jnp.pad(v.astype(F32), (offset, LANES - offset - v.shape[0])).reshape(1, LANES)


def _packed_history(buf, seq_len):
    b, r, c = buf.shape
    return jnp.pad(buf, ((0, 0), (0, seq_len - r), (0, 0))).reshape(b * seq_len, c)


def _ssd_expand(x, g):
    rows = x.shape[0]
    head = lax.broadcasted_iota(jnp.int32, (rows, SSD_GROUP_INNER), 1) >> _log2(SSD_HEADDIM)
    h0 = g * SSD_GROUP_HEADS
    out = jnp.broadcast_to(x[:, h0:h0 + 1], (rows, SSD_GROUP_INNER))
    for r in range(1, SSD_GROUP_HEADS):
        out = jnp.where(head == r, jnp.broadcast_to(x[:, h0 + r:h0 + r + 1], (rows, SSD_GROUP_INNER)), out)
    return out


def _ssd_step_terms(dt_ref, dtb_ref, alog_ref, seq_len):
    dt = jax.nn.softplus(dt_ref[...] + dtb_ref[...])
    la = dt * (-jnp.exp(alog_ref[...]))
    ct = _cumsum_rows(la, _seq_masks(la.shape[0], seq_len)[0])
    return dt, ct


def _ssd_group_intra(g, xs, bg, cg, dt, ct, ct_t, incl):
    rows = xs.shape[0]
    scores = _dot_nt(cg, bg)
    xdt = xs * _ssd_expand(dt, g)
    head = lax.broadcasted_iota(jnp.int32, (rows, SSD_GROUP_INNER), 1) >> _log2(SSD_HEADDIM)
    y = jnp.zeros((rows, SSD_GROUP_INNER), F32)
    for r in range(SSD_GROUP_HEADS):
        h = g * SSD_GROUP_HEADS + r
        diff = ct[:, h:h + 1] - ct_t[h:h + 1, :]
        d = jnp.where(incl, jnp.exp(jnp.where(incl, diff, 0.0)), 0.0)
        a = (scores * d).astype(BF16)
        y = y + _dot(a, jnp.where(head == r, xdt, 0.0).astype(BF16))
    return y, xdt, _ssd_expand(ct, g)


def _ssd_gate_norm(y, z, nw):
    return _rms_rows(y * jax.nn.silu(z.astype(F32))) * nw


def _ssd_prompt_kernel(z_ref, x_ref, bc_ref, dt_ref, cwx_ref, cwbc_ref, cbx_ref, cbbc_ref, dtb_ref, alog_ref,
                       dsk_ref, nw_ref, y_ref, so_ref, px_ref, pbc_ref, st_ref):
    c = pl.program_id(1)
    rows = x_ref.shape[0]
    gs = SSD_GROUPS * SSD_STATE

    @pl.when(c == 0)
    def _():
        px_ref[...] = jnp.zeros(px_ref.shape, F32)
        pbc_ref[...] = jnp.zeros(pbc_ref.shape, F32)
        st_ref[...] = jnp.zeros(st_ref.shape, F32)

    dt, ct = _ssd_step_terms(dt_ref, dtb_ref, alog_ref, rows)
    ct_t = ct.T
    incl, _ = _seq_masks(rows, rows)
    xs_all = jax.nn.silu(_conv_chunk_mxu(x_ref[...], px_ref[...], cwx_ref) + cbx_ref[...])
    bc_all = jax.nn.silu(_conv_chunk_mxu(bc_ref[...], pbc_ref[...], cwbc_ref) + cbbc_ref[...]).astype(BF16)

    for g in range(SSD_GROUPS):
        ci = slice(g * SSD_GROUP_INNER, (g + 1) * SSD_GROUP_INNER)
        xs = xs_all[:, ci]
        bg = bc_all[:, g * SSD_STATE:(g + 1) * SSD_STATE]
        cg = bc_all[:, gs + g * SSD_STATE:gs + (g + 1) * SSD_STATE]
        y, xdt, ctc = _ssd_group_intra(g, xs, bg, cg, dt, ct, ct_t, incl)
        st = st_ref[g]
        y = y + jnp.exp(ctc) * _dot(cg, st.astype(BF16)) + dsk_ref[:, ci] * xs
        last = ctc[rows - 1:rows, :]
        st_ref[g] = st * jnp.exp(last) + _dot_tn(bg, (xdt * jnp.exp(last - ctc)).astype(BF16))
        y_ref[:, ci] = _ssd_gate_norm(y, z_ref[:, ci], nw_ref[:, ci]).astype(y_ref.dtype)

    px_ref[...] = _last_rows(x_ref)
    pbc_ref[...] = _last_rows(bc_ref)

    @pl.when(c == pl.num_programs(1) - 1)
    def _():
        for g in range(SSD_GROUPS):
            hs = slice(g * SSD_GROUP_HEADS, (g + 1) * SSD_GROUP_HEADS)
            so_ref[0, hs] = st_ref[g].T.reshape(SSD_GROUP_HEADS, SSD_HEADDIM, SSD_STATE)


def _ssd_prompt(proj, dtp, prm, batch, seq):
    t = min(SSD_CHUNK, seq)
    nc = seq // t
    row = lambda b, c: b * nc + c
    half = SSD_INNER
    full = lambda shape: pl.BlockSpec(shape, lambda b, c: (0,) * len(shape))
    return pl.pallas_call(
        _ssd_prompt_kernel,
        grid=(batch, nc),
        in_specs=[
            pl.BlockSpec((t, half), lambda b, c: (row(b, c), 0)),
            pl.BlockSpec((t, half), lambda b, c: (row(b, c), 1)),
            pl.BlockSpec((t, half), lambda b, c: (row(b, c), 2)),
            pl.BlockSpec((t, LANES), lambda b, c: (row(b, c), 0)),
            pl.BlockSpec((CONV_W, half), lambda b, c: (0, 0)),
            pl.BlockSpec((CONV_W, half), lambda b, c: (0, 1)),
            pl.BlockSpec((1, half), lambda b, c: (0, 0)),
            pl.BlockSpec((1, half), lambda b, c: (0, 1)),
            full((1, LANES)), full((1, LANES)), full((1, half)), full((1, half)),
        ],
        out_specs=[
            pl.BlockSpec((t, half), lambda b, c: (row(b, c), 0)),
            pl.BlockSpec((1, SSD_HEADS, SSD_HEADDIM, SSD_STATE), lambda b, c: (b, 0, 0, 0)),
        ],
        out_shape=[
            jax.ShapeDtypeStruct((batch * seq, half), BF16),
            jax.ShapeDtypeStruct((batch, SSD_HEADS, SSD_HEADDIM, SSD_STATE), F32),
        ],
        scratch_shapes=[
            pltpu.VMEM((HALO, half), F32),
            pltpu.VMEM((HALO, half), F32),
            pltpu.VMEM((SSD_GROUPS, SSD_STATE, SSD_GROUP_INNER), F32),
        ],
        compiler_params=_params("parallel", "arbitrary"),
        name="ssd_prompt",
    )(proj, proj, proj, dtp, prm["cw"], prm["cw"], prm["cb"], prm["cb"], prm["dtb"], prm["alog"],
      prm["dsk"], prm["nw"])


SSD_SAMPLE_SEQS = 4


def _ssd_sample_kernel(z_ref, x_ref, bc_ref, hx_ref, hbc_ref, dt_ref, cwx_ref, cwbc_ref, cbx_ref, cbbc_ref,
                       dtb_ref, alog_ref, dsk_ref, nw_ref, s0_ref, y_ref, so_ref,
                       acc_ref, ect_ref, xdw_ref, b_ref, c_ref, ct_ref, *, seq_len):
    j = pl.program_id(1)
    rows = x_ref.shape[0]
    gs = SSD_GROUPS * SSD_STATE

    @pl.when(j == 0)
    def _():
        dt, ct = _ssd_step_terms(dt_ref, dtb_ref, alog_ref, seq_len)
        ct_ref[...] = ct
        ct_t = ct.T
        incl, _ = _seq_masks(rows, seq_len)
        for g in range(SSD_GROUPS):
            ci = slice(g * SSD_GROUP_INNER, (g + 1) * SSD_GROUP_INNER)
            cb_ = slice(g * SSD_STATE, (g + 1) * SSD_STATE)
            cc_ = slice(gs + g * SSD_STATE, gs + (g + 1) * SSD_STATE)
            xs = jax.nn.silu(_conv_packed(x_ref[:, ci], hx_ref[:, ci], cwx_ref, ci, seq_len) + cbx_ref[:, ci])
            bg = jax.nn.silu(_conv_packed(bc_ref[:, cb_], hbc_ref[:, cb_], cwbc_ref, cb_, seq_len)
                             + cbbc_ref[:, cb_]).astype(BF16)
            cg = jax.nn.silu(_conv_packed(bc_ref[:, cc_], hbc_ref[:, cc_], cwbc_ref, cc_, seq_len)
                             + cbbc_ref[:, cc_]).astype(BF16)
            y, xdt, ctc = _ssd_group_intra(g, xs, bg, cg, dt, ct, ct_t, incl)
            acc_ref[:, ci] = y + dsk_ref[:, ci] * xs
            ect_ref[:, ci] = jnp.exp(ctc)
            xdw_ref[:, ci] = xdt * jnp.exp(_last_of_seq(ctc, seq_len) - ctc)
            b_ref[:, cb_] = bg
            c_ref[:, cb_] = cg

    row_seq = lax.broadcasted_iota(jnp.int32, (rows, 1), 0) >> _log2(seq_len)
    for n in range(SSD_SAMPLE_SEQS):
        seq = j * SSD_SAMPLE_SEQS + n
        mine = row_seq == seq
        last_row = seq * seq_len + (seq_len - 1)
        for g in range(SSD_GROUPS):
            ci = slice(g * SSD_GROUP_INNER, (g + 1) * SSD_GROUP_INNER)
            cb_ = slice(g * SSD_STATE, (g + 1) * SSD_STATE)
            hs = slice(g * SSD_GROUP_HEADS, (g + 1) * SSD_GROUP_HEADS)
            sg = s0_ref[n, hs].reshape(SSD_GROUP_INNER, SSD_STATE)
            y_state = ect_ref[:, ci] * _dot_nt(c_ref[:, cb_], sg.astype(BF16))
            acc_ref[:, ci] += jnp.where(mine, y_state, 0.0)
            upd = _dot_tn(jnp.where(mine, xdw_ref[:, ci], 0.0).astype(BF16), b_ref[:, cb_])
            for r in range(SSD_GROUP_HEADS):
                h = g * SSD_GROUP_HEADS + r
                keep = jnp.exp(ct_ref[pl.ds(last_row, 1), pl.ds(h, 1)])
                so_ref[n, h] = s0_ref[n, h] * keep + upd[r * SSD_HEADDIM:(r + 1) * SSD_HEADDIM, :]

    @pl.when(j == pl.num_programs(1) - 1)
    def _():
        for g in range(SSD_GROUPS):
            ci = slice(g * SSD_GROUP_INNER, (g + 1) * SSD_GROUP_INNER)
            y_ref[:, ci] = _ssd_gate_norm(acc_ref[:, ci], z_ref[:, ci], nw_ref[:, ci]).astype(y_ref.dtype)


def _ssd_sample(proj, dtp, hist, prm, s0, seq_len):
    m = proj.shape[0]
    rows = SAMPLE_TILE
    nb = SSD_SAMPLE_SEQS
    steps = rows // seq_len // nb
    half = SSD_INNER
    full = lambda shape: pl.BlockSpec(shape, lambda i, j: (0,) * len(shape))
    state_spec = pl.BlockSpec((nb, SSD_HEADS, SSD_HEADDIM, SSD_STATE), lambda i, j: (i * steps + j, 0, 0, 0))
    return pl.pallas_call(
        functools.partial(_ssd_sample_kernel, seq_len=seq_len),
        grid=(m // rows, steps),
        in_specs=[
            pl.BlockSpec((rows, half), lambda i, j: (i, 0)),
            pl.BlockSpec((rows, half), lambda i, j: (i, 1)),
            pl.BlockSpec((rows, half), lambda i, j: (i, 2)),
            pl.BlockSpec((rows, half), lambda i, j: (i, 0)),
            pl.BlockSpec((rows, half), lambda i, j: (i, 1)),
            pl.BlockSpec((rows, LANES), lambda i, j: (i, 0)),
            pl.BlockSpec((CONV_W, half), lambda i, j: (0, 0)),
            pl.BlockSpec((CONV_W, half), lambda i, j: (0, 1)),
            pl.BlockSpec((1, half), lambda i, j: (0, 0)),
            pl.BlockSpec((1, half), lambda i, j: (0, 1)),
            full((1, LANES)), full((1, LANES)), full((1, half)), full((1, half)),
            state_spec,
        ],
        out_specs=[pl.BlockSpec((rows, half), lambda i, j: (i, 0)), state_spec],
        out_shape=[
            jax.ShapeDtypeStruct((m, half), BF16),
            jax.ShapeDtypeStruct(s0.shape, F32),
        ],
        scratch_shapes=[
            pltpu.VMEM((rows, half), F32),
            pltpu.VMEM((rows, half), F32),
            pltpu.VMEM((rows, half), F32),
            pltpu.VMEM((rows, SSD_GROUPS * SSD_STATE), BF16),
            pltpu.VMEM((rows, SSD_GROUPS * SSD_STATE), BF16),
            pltpu.VMEM((rows, LANES), F32),
        ],
        compiler_params=_params("parallel", "arbitrary"),
        name="ssd_sample",
    )(proj, proj, proj, hist, hist, dtp, prm["cw"], prm["cw"], prm["cb"], prm["cb"], prm["dtb"], prm["alog"],
      prm["dsk"], prm["nw"], s0)


def _ssd_layer(x, s0, cbuf, norm_w, w_main, w_dt, prm, *, batch, seq, sample):
    proj = _rms_proj(x, norm_w, w_main, out_dtype=F32 if sample else BF16, tn=1024)
    dtp = _rms_proj(x, norm_w, w_dt, out_dtype=F32, tn=LANES)
    conv_cols = proj[:, SSD_INNER:].reshape(batch, seq, -1)
    conv_new = conv_cols[:, seq - (CONV_W - 1):, :].astype(F32)
    if sample:
        y, s_new = _ssd_sample(proj, dtp, _packed_history(cbuf, seq), prm, s0, seq)
    else:
        y, s_new = _ssd_prompt(proj, dtp, prm, batch, seq)
    return y, s_new, conv_new


def _l2norm_rows(x):
    return x * lax.rsqrt(jnp.sum(x * x, axis=-1, keepdims=True) + RMS_EPS)


def _gdn_gates(ba_ref, dtb_ref, alog_ref, seq_len):
    ba = ba_ref[...]
    beta = jax.nn.sigmoid(ba)
    g = -jnp.exp(alog_ref[...]) * jax.nn.softplus(ba + dtb_ref[...])
    gc = _cumsum_rows(g, _seq_masks(g.shape[0], seq_len)[0])
    return beta, gc, _last_of_seq(gc, seq_len)


INV_BASE_BLOCK = 16
GDN_PROMPT_KEY_HEADS = 4


def _unit_lower_inverse(ls, seq_len):
    rows = ls[0].shape[0]
    ti = lax.broadcasted_iota(jnp.int32, (rows, rows), 0)
    si = lax.broadcasted_iota(jnp.int32, (rows, rows), 1)
    base = min(INV_BASE_BLOCK, seq_len)
    diag_block = (ti >> _log2(base)) == (si >> _log2(base))
    ns = [jnp.where(diag_block, -l, 0.0) for l in ls]
    eye = jnp.where(ti == si, 1.0, 0.0)
    xs = [eye + n for n in ns]
    for _ in range(_log2(base) - 1):
        n16s = [n.astype(BF16) for n in ns]
        ns = [_dot(n16, n16) for n16 in n16s]
        xs = [x + _dot(x.astype(BF16), n.astype(BF16)) for x, n in zip(xs, ns)]
    blk = base
    while blk < seq_len:
        lower_left = ((ti >> _log2(2 * blk)) == (si >> _log2(2 * blk))) & ((ti >> _log2(blk)) != (si >> _log2(blk)))
        cs = [jnp.where(lower_left, l, 0.0).astype(BF16) for l in ls]
        x16s = [x.astype(BF16) for x in xs]
        ts = [_dot(x16, c).astype(BF16) for x16, c in zip(x16s, cs)]
        xs = [x - _dot(t, x16) for x, t, x16 in zip(xs, ts, x16s)]
        blk *= 2
    return xs


def _gdn_heads_intra(qs, ks, kks, qks, vs, beta_all, gc_all, gl_all, gc_t, heads, incl, strict, seq_len):
    rows = qs[0].shape[0]
    betas, gccs, gcls, ds = [], [], [], []
    for head in heads:
        if isinstance(head, int):
            beta = beta_all[:, head:head + 1]
            gcc = gc_all[:, GDN_VH + head:GDN_VH + head + 1]
            gcl = gl_all[:, GDN_VH + head:GDN_VH + head + 1]
            gcr = gc_t[GDN_VH + head:GDN_VH + head + 1, :]
        else:
            lane = lax.broadcasted_iota(jnp.int32, (rows, LANES), 1)
            sub = lax.broadcasted_iota(jnp.int32, (LANES, rows), 0)
            beta = jnp.sum(jnp.where(lane == head, beta_all, 0.0), axis=1, keepdims=True)
            gcc = jnp.sum(jnp.where(lane == GDN_VH + head, gc_all, 0.0), axis=1, keepdims=True)
            gcl = jnp.sum(jnp.where(lane == GDN_VH + head, gl_all, 0.0), axis=1, keepdims=True)
            gcr = jnp.sum(jnp.where(sub == GDN_VH + head, gc_t, 0.0), axis=0, keepdims=True)
        betas.append(beta)
        gccs.append(gcc)
        gcls.append(gcl)
        ds.append(jnp.where(incl, jnp.exp(jnp.where(incl, gcc - gcr, 0.0)), 0.0))
    key = lambda i: i // GDN_REP
    n = len(heads)
    ps = _unit_lower_inverse(
        [jnp.where(strict, (betas[i] * kks[key(i)]) * ds[i], 0.0) for i in range(n)], seq_len)
    egcs = [jnp.exp(g) for g in gccs]
    rhss = [jnp.concatenate([vs[i] * betas[i], ks[key(i)] * (betas[i] * egcs[i])], axis=1).astype(BF16)
            for i in range(n)]
    sols = [_dot(ps[i].astype(BF16), rhss[i]) for i in range(n)]
    us = [s[:, :GDN_DV] for s in sols]
    ws = [s[:, GDN_DV:].astype(BF16) for s in sols]
    attns = [(qks[key(i)] * ds[i]).astype(BF16) for i in range(n)]
    qds = [(qs[key(i)] * egcs[i]).astype(BF16) for i in range(n)]
    kds = [(ks[key(i)] * jnp.exp(gcls[i] - gccs[i])).astype(BF16) for i in range(n)]
    return us, ws, attns, qds, kds, gccs


def _gdn_gate_norm(o, z, nw):
    return _rms_rows(o) * nw * jax.nn.silu(z.astype(F32))


def _gdn_prompt_kernel(q_ref, k_ref, v_ref, z_ref, ba_ref, cwq_ref, cwk_ref, cwv_ref, dtb_ref, alog_ref, nw_ref,
                       o_ref, so_ref, pq_ref, pk_ref, pv_ref, s_ref):
    c = pl.program_id(1)
    rows = q_ref.shape[0]

    @pl.when(c == 0)
    def _():
        for prev in (pq_ref, pk_ref, pv_ref):
            prev[...] = jnp.zeros(prev.shape, F32)
        s_ref[...] = jnp.zeros(s_ref.shape, F32)

    beta_all, gc_all, gl_all = _gdn_gates(ba_ref, dtb_ref, alog_ref, rows)
    gc_t = gc_all.T
    incl, strict = _seq_masks(rows, rows)

    for kh0 in range(0, GDN_KH, GDN_PROMPT_KEY_HEADS):
        khs = range(kh0, kh0 + GDN_PROMPT_KEY_HEADS)
        heads = [kh * GDN_REP + r for kh in khs for r in range(GDN_REP)]
        key_cols = [slice(kh * GDN_DK, (kh + 1) * GDN_DK) for kh in khs]
        val_cols = [slice(h * GDN_DV, (h + 1) * GDN_DV) for h in heads]
        qs = [_l2norm_rows(jax.nn.silu(_conv_chunk(q_ref[:, c_].astype(F32), pq_ref[:, c_], cwq_ref, c_)))
              * (GDN_DK ** -0.5) for c_ in key_cols]
        ks = [_l2norm_rows(jax.nn.silu(_conv_chunk(k_ref[:, c_].astype(F32), pk_ref[:, c_], cwk_ref, c_)))
              for c_ in key_cols]
        vs = [jax.nn.silu(_conv_chunk(v_ref[:, c_].astype(F32), pv_ref[:, c_], cwv_ref, c_)) for c_ in val_cols]
        q16s = [q.astype(BF16) for q in qs]
        k16s = [k.astype(BF16) for k in ks]
        kks = [_dot_nt(k16, k16) for k16 in k16s]
        qks = [_dot_nt(q16, k16) for q16, k16 in zip(q16s, k16s)]
        us, ws, attns, qds, kds, gccs = _gdn_heads_intra(
            qs, ks, kks, qks, vs, beta_all, gc_all, gl_all, gc_t, heads, incl, strict, rows)
        n = len(heads)
        s_olds = [s_ref[h] for h in heads]
        s16s = [s.astype(BF16) for s in s_olds]
        v_news = [us[i] - _dot(ws[i], s16s[i]) for i in range(n)]
        vn16s = [v.astype(BF16) for v in v_news]
        os_ = [_dot(qds[i], s16s[i]) + _dot(attns[i], vn16s[i]) for i in range(n)]
        for i, h in enumerate(heads):
            s_ref[h] = s_olds[i] * jnp.exp(gccs[i][rows - 1:rows, :]) + _dot_tn(kds[i], vn16s[i])
            o_ref[:, val_cols[i]] = _gdn_gate_norm(os_[i], z_ref[:, val_cols[i]], nw_ref[...]).astype(o_ref.dtype)

    pq_ref[...] = _last_rows(q_ref)
    pk_ref[...] = _last_rows(k_ref)
    pv_ref[...] = _last_rows(v_ref)

    @pl.when(c == pl.num_programs(1) - 1)
    def _():
        so_ref[0] = s_ref[...]


def _gdn_col_specs(rows, row_map):
    kcol = GDN_KD // GDN_DK
    vcol = 2 * GDN_KD // (GDN_REP * GDN_DV)
    return [
        pl.BlockSpec((rows, GDN_DK), lambda *ix: (row_map(*ix), ix[1])),
        pl.BlockSpec((rows, GDN_DK), lambda *ix: (row_map(*ix), kcol + ix[1])),
        pl.BlockSpec((rows, GDN_REP * GDN_DV), lambda *ix: (row_map(*ix), vcol + ix[1])),
    ]


def _gdn_prompt(proj, ba, prm, batch, seq):
    t = min(GDN_CHUNK, seq)
    nc = seq // t
    row = lambda b, c: b * nc + c
    full = lambda shape: pl.BlockSpec(shape, lambda b, c: (0,) * len(shape))
    return pl.pallas_call(
        _gdn_prompt_kernel,
        grid=(batch, nc),
        in_specs=[
            pl.BlockSpec((t, GDN_KD), lambda b, c: (row(b, c), 0)),
            pl.BlockSpec((t, GDN_KD), lambda b, c: (row(b, c), 1)),
            pl.BlockSpec((t, GDN_V), lambda b, c: (row(b, c), 1)),
            pl.BlockSpec((t, GDN_V), lambda b, c: (row(b, c), 2)),
            pl.BlockSpec((t, LANES), lambda b, c: (row(b, c), 0)),
            pl.BlockSpec((CONV_W, GDN_KD), lambda b, c: (0, 0)),
            pl.BlockSpec((CONV_W, GDN_KD), lambda b, c: (0, 1)),
            pl.BlockSpec((CONV_W, GDN_V), lambda b, c: (0, 1)),
            full((1, LANES)), full((1, LANES)), full((1, GDN_DV)),
        ],
        out_specs=[
            pl.BlockSpec((t, GDN_V), lambda b, c: (row(b, c), 0)),
            pl.BlockSpec((1, GDN_VH, GDN_DK, GDN_DV), lambda b, c: (b, 0, 0, 0)),
        ],
        out_shape=[
            jax.ShapeDtypeStruct((batch * seq, GDN_V), BF16),
            jax.ShapeDtypeStruct((batch, GDN_VH, GDN_DK, GDN_DV), F32),
        ],
        scratch_shapes=[
            pltpu.VMEM((HALO, GDN_KD), F32),
            pltpu.VMEM((HALO, GDN_KD), F32),
            pltpu.VMEM((HALO, GDN_V), F32),
            pltpu.VMEM((GDN_VH, GDN_DK, GDN_DV), F32),
        ],
        compiler_params=_params("parallel", "arbitrary"),
        name="gdn_prompt",
    )(proj, proj, proj, proj, ba, prm["cw"], prm["cw"], prm["cw"], prm["dtb"], prm["alog"], prm["nw"])


GDN_SAMPLE_SEQS = 8


def _gdn_sample_kernel(q_ref, k_ref, v_ref, z_ref, ba_ref, hq_ref, hk_ref, hv_ref, cwq_ref, cwk_ref, cwv_ref,
                       dtb_ref, alog_ref, nw_ref, s0_ref, o_ref, so_ref,
                       u_ref, w_ref, attn_ref, qd_ref, kd_ref, gcc_ref, vnew_ref, qs_ref, *, seq_len):
    kh = pl.program_id(1)
    j = pl.program_id(2)
    rows = q_ref.shape[0]

    @pl.when(j == 0)
    def _():
        one = slice(0, GDN_DK)
        q = _l2norm_rows(jax.nn.silu(_conv_packed(q_ref[...], hq_ref[...], cwq_ref, one, seq_len))) * (GDN_DK ** -0.5)
        k = _l2norm_rows(jax.nn.silu(_conv_packed(k_ref[...], hk_ref[...], cwk_ref, one, seq_len)))
        v_all = jax.nn.silu(_conv_packed(v_ref[...], hv_ref[...], cwv_ref, slice(0, GDN_REP * GDN_DV), seq_len))
        beta_all, gc_all, gl_all = _gdn_gates(ba_ref, dtb_ref, alog_ref, seq_len)
        gc_t = gc_all.T
        incl, strict = _seq_masks(rows, seq_len)
        q16, k16 = q.astype(BF16), k.astype(BF16)
        kk = _dot_nt(k16, k16)
        qk = _dot_nt(q16, k16)
        vnew_ref[...] = jnp.zeros(vnew_ref.shape, F32)
        qs_ref[...] = jnp.zeros(qs_ref.shape, F32)
        vs = [v_all[:, r * GDN_DV:(r + 1) * GDN_DV] for r in range(GDN_REP)]
        heads = [kh * GDN_REP + r for r in range(GDN_REP)]
        us, ws, attns, qds, kds, gccs = _gdn_heads_intra(
            [q], [k], [kk], [qk], vs, beta_all, gc_all, gl_all, gc_t, heads, incl, strict, seq_len)
        for r in range(GDN_REP):
            u_ref[r] = us[r]
            w_ref[r] = ws[r]
            attn_ref[r] = attns[r]
            qd_ref[r] = qds[r]
            kd_ref[r] = kds[r]
            gcc_ref[r] = jnp.broadcast_to(gccs[r], (rows, LANES))

    row_seq = lax.broadcasted_iota(jnp.int32, (rows, GDN_DV), 0) >> _log2(seq_len)
    for n in range(GDN_SAMPLE_SEQS):
        seq = j * GDN_SAMPLE_SEQS + n
        mine = row_seq == seq
        last_row = seq * seq_len + (seq_len - 1)
        for r in range(GDN_REP):
            s_old = s0_ref[n, r]
            s16 = s_old.astype(BF16)
            v_new = u_ref[r] - _dot(w_ref[r], s16)
            vnew_ref[r] = jnp.where(mine, v_new, vnew_ref[r])
            qs_ref[r] = jnp.where(mine, _dot(qd_ref[r], s16), qs_ref[r])
            kd = jnp.where(mine, kd_ref[r].astype(F32), 0.0).astype(BF16)
            keep = jnp.exp(gcc_ref[r, pl.ds(last_row, 1), :])
            so_ref[n, r] = s_old * keep + _dot_tn(kd, v_new.astype(BF16))

    @pl.when(j == pl.num_programs(2) - 1)
    def _():
        for r in range(GDN_REP):
            cv = slice(r * GDN_DV, (r + 1) * GDN_DV)
            o = qs_ref[r] + _dot(attn_ref[r], vnew_ref[r].astype(BF16))
            o_ref[:, cv] = _gdn_gate_norm(o, z_ref[:, cv], nw_ref[...]).astype(o_ref.dtype)


def _gdn_sample(proj, ba, hist, prm, s0, seq_len):
    m = proj.shape[0]
    rows = SAMPLE_TILE
    nb = GDN_SAMPLE_SEQS
    steps = rows // seq_len // nb
    row_map = lambda i, kh, j: i
    zcol = (2 * GDN_KD + GDN_V) // (GDN_REP * GDN_DV)
    pair = GDN_REP * GDN_DV
    full = lambda shape: pl.BlockSpec(shape, lambda i, kh, j: (0,) * len(shape))
    state_spec = pl.BlockSpec((nb, GDN_REP, GDN_DK, GDN_DV), lambda i, kh, j: (i * steps + j, kh, 0, 0))
    head_f32 = pltpu.VMEM((GDN_REP, rows, GDN_DV), F32)
    head_bf16 = pltpu.VMEM((GDN_REP, rows, GDN_DV), BF16)
    return pl.pallas_call(
        functools.partial(_gdn_sample_kernel, seq_len=seq_len),
        grid=(m // rows, GDN_KH, steps),
        in_specs=_gdn_col_specs(rows, row_map) + [
            pl.BlockSpec((rows, pair), lambda i, kh, j: (i, zcol + kh)),
            pl.BlockSpec((rows, LANES), lambda i, kh, j: (i, 0)),
        ] + _gdn_col_specs(rows, row_map) + _gdn_col_specs(CONV_W, lambda i, kh, j: 0) + [
            full((1, LANES)), full((1, LANES)), full((1, GDN_DV)), state_spec],
        out_specs=[pl.BlockSpec((rows, pair), lambda i, kh, j: (i, kh)), state_spec],
        out_shape=[
            jax.ShapeDtypeStruct((m, GDN_V), BF16),
            jax.ShapeDtypeStruct(s0.shape, F32),
        ],
        scratch_shapes=[head_f32, head_bf16, head_bf16, head_bf16, head_bf16, head_f32, head_f32, head_f32],
        compiler_params=_params("parallel", "parallel", "arbitrary"),
        name="gdn_sample",
    )(proj, proj, proj, proj, ba, hist, hist, hist, prm["cw"], prm["cw"], prm["cw"], prm["dtb"], prm["alog"],
      prm["nw"], s0)


def _gdn_layer(x, s0, cbuf, norm_w, w_main, w_ba, prm, *, batch, seq, sample):
    proj = _rms_proj(x, norm_w, w_main, out_dtype=F32 if sample else BF16, tn=1024)
    ba = _rms_proj(x, norm_w, w_ba, out_dtype=F32, tn=LANES)
    conv_cols = proj[:, :2 * GDN_KD + GDN_V].reshape(batch, seq, -1)
    conv_new = conv_cols[:, seq - (CONV_W - 1):, :].astype(F32)
    if sample:
        o, s_new = _gdn_sample(proj, ba, _packed_history(cbuf, seq), prm, s0, seq)
    else:
        o, s_new = _gdn_prompt(proj, ba, prm, batch, seq)
    return o, s_new, conv_new


def _layer_of(stack, j):
    if stack is None:
        return None
    return stack.reshape(stack.shape[1:]) if stack.shape[0] == 1 else stack[j]


def _stack(layers):
    return layers[0].reshape((1,) + layers[0].shape) if len(layers) == 1 else jnp.stack(layers)


def _trunk(x, weights, states, *, batch, seq, pos0, sample):
    s_ret, s_ssd, c_ssd, s_gdn, c_gdn = states
    n_ssd, n_ssdc, n_gdn, n_gdnc = [], [], [], []
    ret_stack = None
    depth = weights["norm_mix"].shape[0]
    n_ret_layers = weights["ret_w_in"].shape[0]
    for i in range(depth):
        kind, jdx = i % 3, i // 3
        nm = weights["norm_mix"][i]
        if kind == 0:
            y, ret_stack = _ret_layer(x, s_ret, ret_stack, jdx, n_ret_layers, nm, weights["ret_w_in"][jdx],
                                      batch=batch, seq=seq, pos0=pos0, sample=sample)
            w_out = weights["ret_w_out"][jdx]
        elif kind == 1:
            y, s, cnew = _ssd_layer(x, _layer_of(s_ssd, jdx), _layer_of(c_ssd, jdx), nm, weights["ssd_w_main"][jdx],
                                    weights["ssd_w_dt"][jdx], weights["ssd_prm"][jdx],
                                    batch=batch, seq=seq, sample=sample)
            w_out = weights["ssd_w_out"][jdx]
            n_ssd.append(s)
            n_ssdc.append(cnew)
        else:
            y, s, cnew = _gdn_layer(x, _layer_of(s_gdn, jdx), _layer_of(c_gdn, jdx), nm, weights["gdn_w_main"][jdx],
                                    weights["gdn_w_ba"][jdx], weights["gdn_prm"][jdx],
                                    batch=batch, seq=seq, sample=sample)
            w_out = weights["gdn_w_out"][jdx]
            n_gdn.append(s)
            n_gdnc.append(cnew)
        final_w = weights["norm_final"] if i == depth - 1 else None
        x = _mix_mlp(y, w_out, x, weights["norm_mlp"][i], weights["mlp_w_up"][i], weights["mlp_w_down"][i], final_w)
    return (x.reshape(batch, seq, -1), ret_stack, _stack(n_ssd), _stack(n_ssdc), _stack(n_gdn), _stack(n_gdnc))


def _prepare_weights(norm_mix, norm_mlp, norm_final, ret_w_in, ret_w_out,
                     ssd_w_in, ssd_conv_w, ssd_conv_b, ssd_dt_bias, ssd_a_log, ssd_d, ssd_norm, ssd_w_out,
                     gdn_w_in, gdn_conv_w, gdn_dt_bias, gdn_a_log, gdn_norm, gdn_w_out, mlp_w_up, mlp_w_down):
    n_ssd, n_gdn = ssd_w_in.shape[0], gdn_w_in.shape[0]
    main = 3 * SSD_INNER

    def small(w):
        return jnp.pad(w[:, main:], ((0, 0), (0, LANES - (w.shape[1] - main)))).astype(BF16)

    ssd_prm = [dict(cw=ssd_conv_w[i], cb=ssd_conv_b[i].reshape(1, -1), dtb=_pad_lanes(ssd_dt_bias[i]),
                    alog=_pad_lanes(ssd_a_log[i]), dsk=jnp.repeat(ssd_d[i], SSD_HEADDIM).reshape(1, -1),
                    nw=ssd_norm[i].reshape(1, -1)) for i in range(n_ssd)]
    gdn_prm = [dict(cw=gdn_conv_w[i], dtb=_pad_lanes(gdn_dt_bias[i], GDN_VH), alog=_pad_lanes(gdn_a_log[i], GDN_VH),
                    nw=gdn_norm[i].reshape(1, -1)) for i in range(n_gdn)]
    return dict(
        norm_mix=norm_mix, norm_mlp=norm_mlp, norm_final=norm_final,
        ret_w_in=ret_w_in.astype(BF16), ret_w_out=ret_w_out.astype(BF16),
        ssd_w_main=[ssd_w_in[i, :, :main].astype(BF16) for i in range(n_ssd)],
        ssd_w_dt=[small(ssd_w_in[i]) for i in range(n_ssd)],
        ssd_prm=ssd_prm, ssd_w_out=ssd_w_out.astype(BF16),
        gdn_w_main=[gdn_w_in[i, :, :main].astype(BF16) for i in range(n_gdn)],
        gdn_w_ba=[small(gdn_w_in[i]) for i in range(n_gdn)],
        gdn_prm=gdn_prm, gdn_w_out=gdn_w_out.astype(BF16),
        mlp_w_up=mlp_w_up.astype(BF16), mlp_w_down=mlp_w_down.astype(BF16),
    )


def kernel(x_prompt, x_sample, state_ret, state_ssd, state_ssd_conv, state_gdn, state_gdn_conv, norm_mix, norm_mlp,
           norm_final, ret_w_in, ret_w_out, ssd_w_in, ssd_conv_w, ssd_conv_b, ssd_dt_bias, ssd_a_log, ssd_d, ssd_norm,
           ssd_w_out, gdn_w_in, gdn_conv_w, gdn_dt_bias, gdn_a_log, gdn_norm, gdn_w_out, mlp_w_up, mlp_w_down):
    weights = _prepare_weights(norm_mix, norm_mlp, norm_final, ret_w_in, ret_w_out, ssd_w_in, ssd_conv_w, ssd_conv_b,
                               ssd_dt_bias, ssd_a_log, ssd_d, ssd_norm, ssd_w_out, gdn_w_in, gdn_conv_w, gdn_dt_bias,
                               gdn_a_log, gdn_norm, gdn_w_out, mlp_w_up, mlp_w_down)
    bp, lp, d = x_prompt.shape
    bs, ls, _ = x_sample.shape
    assert lp % RET_CHUNK == 0 and lp % SSD_CHUNK == 0 and lp % GDN_CHUNK == 0
    assert SAMPLE_TILE % ls == 0 and (bs * ls) % SAMPLE_TILE == 0 and ls >= CONV_W - 1
    prompt = _trunk(x_prompt.reshape(bp * lp, d), weights, (None,) * 5, batch=bp, seq=lp, pos0=0, sample=False)
    sample = _trunk(x_sample.reshape(bs * ls, d), weights,
                    (state_ret, state_ssd, state_ssd_conv, state_gdn, state_gdn_conv),
                    batch=bs, seq=ls, pos0=PAST_LEN, sample=True)
    return (prompt[0], sample[0]) + prompt[1:] + sample[1:]
```

```python
import functools
import math

import jax
import jax.numpy as jnp
from jax import lax
from jax.experimental import pallas as pl
from jax.experimental.pallas import tpu as pltpu

F32 = jnp.float32
BF16 = jnp.bfloat16

D_MODEL = 1024
RET_HEADS = 4
RET_DK = 256
RET_DV = 512
RET_QK = RET_HEADS * RET_DK
RET_V = RET_HEADS * RET_DV
ROPE_BASE = 10000.0
SSD_INNER = 2048
SSD_HEADDIM = 64
SSD_HEADS = 32
SSD_GROUPS = 8
SSD_STATE = 128
SSD_GROUP_HEADS = SSD_HEADS // SSD_GROUPS
SSD_GROUP_INNER = SSD_INNER // SSD_GROUPS
GDN_KH = 8
GDN_VH = 16
GDN_DK = 128
GDN_DV = 128
GDN_KD = GDN_KH * GDN_DK
GDN_V = GDN_VH * GDN_DV
GDN_REP = GDN_VH // GDN_KH
CONV_W = 4
D_FF = 4096
RMS_EPS = 1e-6
PAST_LEN = 16384

LANES = 128
SUBLANES = 8
VMEM_LIMIT_BYTES = 56 * 1024 * 1024

SAMPLE_TILE = 128
RET_CHUNK = 256
SSD_CHUNK = 128
GDN_CHUNK = 128
HALO = SUBLANES

_NT = (((1,), (1,)), ((), ()))
_TN = (((0,), (0,)), ((), ()))


def _params(*sem):
    return pltpu.CompilerParams(dimension_semantics=sem, vmem_limit_bytes=VMEM_LIMIT_BYTES)


def _dot(a, b):
    return jnp.dot(a, b, preferred_element_type=F32)


def _dot_nt(a, b):
    return lax.dot_general(a, b, _NT, preferred_element_type=F32)


def _dot_tn(a, b):
    return lax.dot_general(a, b, _TN, preferred_element_type=F32)


def _rms_rows(x):
    return x * lax.rsqrt(jnp.mean(x * x, axis=-1, keepdims=True) + RMS_EPS)


def _log2(n):
    k = n.bit_length() - 1
    assert 1 << k == n, n
    return k


def _seq_masks(rows, seq_len):
    t = lax.broadcasted_iota(jnp.int32, (rows, rows), 0)
    s = lax.broadcasted_iota(jnp.int32, (rows, rows), 1)
    sh = _log2(seq_len)
    same = (t >> sh) == (s >> sh)
    return same & (t >= s), same & (t > s)


def _cumsum_rows(x, incl):
    hi = x.astype(BF16)
    r1 = x - hi.astype(F32)
    mid = r1.astype(BF16)
    lo = (r1 - mid.astype(F32)).astype(BF16)
    tri = jnp.where(incl, 1.0, 0.0).astype(BF16)
    parts = _dot(tri, jnp.concatenate([hi, mid, lo], axis=1))
    w = x.shape[1]
    return (parts[:, 2 * w:] + parts[:, w:2 * w]) + parts[:, :w]


DENSE_ROWS = 512


def _resident(shape):
    return pl.BlockSpec(shape, lambda i: (0,) * len(shape), pipeline_mode=pl.Buffered(1))


def _proj_kernel(x_ref, nw_ref, w_ref, cos_ref, sin_ref, o_ref, *, tn, n_rot, k_scale):
    h = (_rms_rows(x_ref[...]) * nw_ref[...]).astype(BF16)
    for j in range(w_ref.shape[1] // tn):
        cols = slice(j * tn, (j + 1) * tn)
        acc = _dot(h, w_ref[:, cols])
        if j >= n_rot:
            o_ref[:, cols] = acc.astype(o_ref.dtype)
            continue
        c, s = cos_ref[...], sin_ref[...]
        scale = 1.0 if j == 0 else k_scale
        half = RET_DK // 2
        for hd in range(tn // RET_DK):
            lo = j * tn + hd * RET_DK
            x1 = acc[:, hd * RET_DK:hd * RET_DK + half]
            x2 = acc[:, hd * RET_DK + half:(hd + 1) * RET_DK]
            o_ref[:, lo:lo + half] = ((x1 * c - x2 * s) * scale).astype(o_ref.dtype)
            o_ref[:, lo + half:lo + RET_DK] = ((x1 * s + x2 * c) * scale).astype(o_ref.dtype)


def _rms_proj(x, norm_w, w_bf16, *, out_dtype, tn, rot=None):
    m, d = x.shape
    n = w_bf16.shape[1]
    tm = min(m, DENSE_ROWS)
    assert m % tm == 0 and n % tn == 0
    if rot is None:
        cos = sin = jnp.zeros((SUBLANES, LANES), F32)
        rot_spec = pl.BlockSpec((SUBLANES, LANES), lambda i: (0, 0))
        n_rot = 0
    else:
        cos, sin = rot
        period = cos.shape[0] // tm
        assert cos.shape[0] % tm == 0 and tn == RET_QK
        rot_spec = pl.BlockSpec((tm, LANES), lambda i: (i % period, 0))
        n_rot = 2
    kern = functools.partial(_proj_kernel, tn=tn, n_rot=n_rot, k_scale=RET_DK ** -0.5)
    return pl.pallas_call(
        kern,
        grid=(m // tm,),
        in_specs=[
            pl.BlockSpec((tm, d), lambda i: (i, 0)),
            _resident((1, d)),
            _resident((d, n)),
            rot_spec, rot_spec,
        ],
        out_specs=pl.BlockSpec((tm, n), lambda i: (i, 0)),
        out_shape=jax.ShapeDtypeStruct((m, n), out_dtype),
        compiler_params=_params("parallel"),
        name="rms_proj",
    )(x, norm_w.reshape(1, d), w_bf16, cos, sin)


MLP_FF_TILE = 1024


def _mix_mlp_kernel(y_ref, wo_ref, x_ref, nw_ref, wu_ref, wd_ref, fw_ref, o_ref, *, final):
    x1 = x_ref[...] + _dot(y_ref[...], wo_ref[...])
    h = (_rms_rows(x1) * nw_ref[...]).astype(BF16)
    acc = x1
    for f in range(wu_ref.shape[1] // MLP_FF_TILE):
        cols = slice(f * MLP_FF_TILE, (f + 1) * MLP_FF_TILE)
        u = _dot(h, wu_ref[:, cols])
        acc = acc + _dot(jnp.square(jnp.maximum(u, 0.0)).astype(BF16), wd_ref[cols, :])
    if final:
        acc = _rms_rows(acc) * fw_ref[...]
    o_ref[...] = acc


def _mix_mlp(y, wo_bf16, x, norm_w, wu_bf16, wd_bf16, final_w=None):
    m, d = x.shape
    k = y.shape[1]
    ff = wu_bf16.shape[1]
    tm = min(m, DENSE_ROWS)
    final = final_w is not None
    fw = (final_w if final else jnp.zeros((d,), F32)).reshape(1, d)
    return pl.pallas_call(
        functools.partial(_mix_mlp_kernel, final=final),
        grid=(m // tm,),
        in_specs=[
            pl.BlockSpec((tm, k), lambda i: (i, 0)),
            _resident((k, d)),
            pl.BlockSpec((tm, d), lambda i: (i, 0)),
            _resident((1, d)),
            _resident((d, ff)),
            _resident((ff, d)),
            _resident((1, d)),
        ],
        out_specs=pl.BlockSpec((tm, d), lambda i: (i, 0)),
        out_shape=jax.ShapeDtypeStruct((m, d), F32),
        compiler_params=_params("parallel"),
        name="mix_mlp",
    )(y, wo_bf16, x, norm_w.reshape(1, d), wu_bf16, wd_bf16, fw)


def _ret_log_gamma(h):
    return math.log1p(-(2.0 ** (-5.0 - h)))


def _ret_decay(rows, seq_len, h):
    lg = _ret_log_gamma(h)
    t = lax.broadcasted_iota(jnp.int32, (rows, rows), 0)
    s = lax.broadcasted_iota(jnp.int32, (rows, rows), 1)
    incl, _ = _seq_masks(rows, seq_len)
    d = jnp.where(incl, jnp.exp(jnp.where(incl, (t - s).astype(F32) * lg, 0.0)), 0.0)
    pos = (lax.broadcasted_iota(jnp.int32, (rows, 1), 0) & (seq_len - 1)).astype(F32)
    return d, pos, lg


def _ret_gate_norm(y, g):
    return jax.nn.silu(g.astype(F32)) * _rms_rows(y)


def _ret_prompt_kernel(q_ref, k_ref, v_ref, g_ref, y_ref, so_ref, s_ref, d_ref):
    c = pl.program_id(1)
    rows = q_ref.shape[0]

    @pl.when(c == 0)
    def _():
        s_ref[...] = jnp.zeros(s_ref.shape, F32)
        for h in range(RET_HEADS):
            d_ref[h] = _ret_decay(rows, rows, h)[0]

    pos = lax.broadcasted_iota(jnp.int32, (rows, 1), 0).astype(F32)
    for h in range(RET_HEADS):
        lg = _ret_log_gamma(h)
        q = q_ref[:, h * RET_DK:(h + 1) * RET_DK]
        k = k_ref[:, h * RET_DK:(h + 1) * RET_DK]
        v = v_ref[:, h * RET_DV:(h + 1) * RET_DV]
        a = (_dot_nt(q, k) * d_ref[h]).astype(BF16)
        s_old = s_ref[h]
        y = _dot(a, v) + jnp.exp((pos + 1.0) * lg) * _dot(q, s_old.astype(BF16))
        kw = (k.astype(F32) * jnp.exp((rows - 1.0 - pos) * lg)).astype(BF16)
        s_ref[h] = s_old * math.exp(rows * lg) + _dot_tn(kw, v)
        y_ref[:, h * RET_DV:(h + 1) * RET_DV] = _ret_gate_norm(
            y, g_ref[:, h * RET_DV:(h + 1) * RET_DV]).astype(y_ref.dtype)

    @pl.when(c == pl.num_programs(1) - 1)
    def _():
        so_ref[0] = s_ref[...]


def _without_arg(fn, idx):
    def wrapped(*refs, **kw):
        return fn(*refs[:idx], *refs[idx + 1:], **kw)
    return wrapped


def _stacked_state_out(n_layers, layer, batch, carried, n_in, index_of, block=(1, RET_HEADS, RET_DK, RET_DV)):
    spec = pl.BlockSpec((None,) + block, lambda *ix: (layer, index_of(*ix), 0, 0, 0))
    shape = jax.ShapeDtypeStruct((n_layers, batch) + block[1:], F32)
    extra_in = [] if carried is None else [pl.BlockSpec(memory_space=pl.ANY)]
    extra_args = [] if carried is None else [carried]
    aliases = {} if carried is None else {n_in: 1}
    return spec, shape, extra_in, extra_args, aliases


def _ret_prompt(proj, batch, seq, layer, n_layers, carried):
    t = min(RET_CHUNK, seq)
    nc = seq // t
    row = lambda b, c: b * nc + c
    n_in = 4
    s_spec, s_shape, extra_in, extra_args, aliases = _stacked_state_out(
        n_layers, layer, batch, carried, n_in, lambda b, c: b)
    kern = _ret_prompt_kernel if carried is None else _without_arg(_ret_prompt_kernel, n_in)
    return pl.pallas_call(
        kern,
        grid=(batch, nc),
        in_specs=[
            pl.BlockSpec((t, RET_QK), lambda b, c: (row(b, c), 0)),
            pl.BlockSpec((t, RET_QK), lambda b, c: (row(b, c), 1)),
            pl.BlockSpec((t, RET_V), lambda b, c: (row(b, c), 1)),
            pl.BlockSpec((t, RET_V), lambda b, c: (row(b, c), 2)),
        ] + extra_in,
        out_specs=[pl.BlockSpec((t, RET_V), lambda b, c: (row(b, c), 0)), s_spec],
        out_shape=[jax.ShapeDtypeStruct((batch * seq, RET_V), BF16), s_shape],
        scratch_shapes=[pltpu.VMEM((RET_HEADS, RET_DK, RET_DV), F32),
                        pltpu.VMEM((RET_HEADS, t, t), F32)],
        input_output_aliases=aliases,
        compiler_params=_params("parallel", "arbitrary"),
        name="ret_prompt",
    )(proj, proj, proj, proj, *extra_args)


def _ret_sample_kernel(q_ref, k_ref, v_ref, g_ref, s0_ref, y_ref, so_ref, acc_ref, *, seq_len):
    j = pl.program_id(1)
    rows = q_ref.shape[0]
    row_seq = lax.broadcasted_iota(jnp.int32, (rows, 1), 0) >> _log2(seq_len)
    mine = row_seq == j

    for h in range(RET_HEADS):
        d, pos, lg = _ret_decay(rows, seq_len, h)
        q = q_ref[:, h * RET_DK:(h + 1) * RET_DK].astype(BF16)
        k = k_ref[:, h * RET_DK:(h + 1) * RET_DK]
        v = v_ref[:, h * RET_DV:(h + 1) * RET_DV].astype(BF16)
        cols = slice(h * RET_DV, (h + 1) * RET_DV)

        @pl.when(j == 0)
        def _():
            a = (_dot_nt(q, k.astype(BF16)) * d).astype(BF16)
            acc_ref[:, cols] = _dot(a, v)

        s_old = s0_ref[0, h]
        y_state = jnp.exp((pos + 1.0) * lg) * _dot(q, s_old.astype(BF16))
        acc_ref[:, cols] += jnp.where(mine, y_state, 0.0)
        kw = jnp.where(mine, k.astype(F32) * jnp.exp((seq_len - 1.0 - pos) * lg), 0.0).astype(BF16)
        so_ref[0, h] = s_old * math.exp(seq_len * lg) + _dot_tn(kw, v)

    @pl.when(j == pl.num_programs(1) - 1)
    def _():
        for h in range(RET_HEADS):
            cols = slice(h * RET_DV, (h + 1) * RET_DV)
            y_ref[:, cols] = _ret_gate_norm(acc_ref[:, cols], g_ref[:, cols]).astype(y_ref.dtype)


def _ret_sample(proj, states, seq_len, layer, carried):
    m = proj.shape[0]
    rows = SAMPLE_TILE
    per_tile = rows // seq_len
    n_layers, batch = states.shape[:2]
    seq_of = lambda i, j: i * per_tile + j
    n_in = 5
    s_spec, s_shape, extra_in, extra_args, aliases = _stacked_state_out(
        n_layers, layer, batch, carried, n_in, seq_of)
    kern = functools.partial(_ret_sample_kernel, seq_len=seq_len)
    if carried is not None:
        kern = _without_arg(kern, n_in)
    return pl.pallas_call(
        kern,
        grid=(m // rows, per_tile),
        in_specs=[
            pl.BlockSpec((rows, RET_QK), lambda i, j: (i, 0)),
            pl.BlockSpec((rows, RET_QK), lambda i, j: (i, 1)),
            pl.BlockSpec((rows, RET_V), lambda i, j: (i, 1)),
            pl.BlockSpec((rows, RET_V), lambda i, j: (i, 2)),
            pl.BlockSpec((None, 1, RET_HEADS, RET_DK, RET_DV), lambda i, j: (layer, seq_of(i, j), 0, 0, 0)),
        ] + extra_in,
        out_specs=[pl.BlockSpec((rows, RET_V), lambda i, j: (i, 0)), s_spec],
        out_shape=[jax.ShapeDtypeStruct((m, RET_V), BF16), s_shape],
        scratch_shapes=[pltpu.VMEM((rows, RET_V), F32)],
        input_output_aliases=aliases,
        compiler_params=_params("parallel", "arbitrary"),
        name="ret_sample",
    )(proj, proj, proj, proj, states, *extra_args)


def _rope_tables(seq, pos0, reps):
    half = RET_DK // 2
    inv = ROPE_BASE ** (-jnp.arange(half, dtype=F32) / half)
    ang = (jnp.arange(seq, dtype=F32) + pos0)[:, None] * inv[None, :]
    return jnp.tile(jnp.cos(ang), (reps, 1)), jnp.tile(jnp.sin(ang), (reps, 1))


def _ret_layer(x, states, carried, layer, n_layers, norm_w, w_in, *, batch, seq, pos0, sample):
    m = x.shape[0]
    if sample:
        rot = _rope_tables(seq, pos0, min(m, DENSE_ROWS) // seq)
    else:
        rot = _rope_tables(seq, pos0, 1)
    proj = _rms_proj(x, norm_w, w_in, out_dtype=F32 if sample else BF16, tn=RET_QK, rot=rot)
    if sample:
        y, stack = _ret_sample(proj, states, seq, layer, carried)
    else:
        y, stack = _ret_prompt(proj, batch, seq, layer, n_layers, carried)
    return y, stack


def _conv_chunk(x, prev, cw_ref, cols):
    def taps(src, lo, hi):
        acc = None
        for tap in range(CONV_W):
            back = CONV_W - 1 - tap
            cur = (src if back == 0 else pltpu.roll(src, back, 0))[lo:hi] * cw_ref[tap:tap + 1, cols]
            acc = cur if acc is None else acc + cur
        return acc

    rows = x.shape[0]
    head = taps(jnp.concatenate([prev, x[0:HALO]], axis=0), HALO, 2 * HALO)
    return jnp.concatenate([head, taps(x, HALO, rows)], axis=0)


def _conv_chunk_mxu(x16, prev, cw_ref):
    rows = x16.shape[0]
    ti = lax.broadcasted_iota(jnp.int32, (rows, rows), 0)
    si = lax.broadcasted_iota(jnp.int32, (rows, rows), 1)
    shifts = jnp.concatenate(
        [jnp.where(ti - si == CONV_W - 1 - tap, 1.0, 0.0) for tap in range(CONV_W - 1)], axis=0).astype(BF16)
    moved = _dot(shifts, x16)
    x = x16.astype(F32)
    acc = moved[0:rows] * cw_ref[0:1, :]
    for tap in range(1, CONV_W - 1):
        acc = acc + moved[tap * rows:(tap + 1) * rows] * cw_ref[tap:tap + 1, :]
    acc = acc + x * cw_ref[CONV_W - 1:CONV_W, :]
    window = jnp.concatenate([prev, x[0:HALO]], axis=0)
    head = None
    for tap in range(CONV_W):
        back = CONV_W - 1 - tap
        cur = (window if back == 0 else pltpu.roll(window, back, 0))[HALO:2 * HALO] * cw_ref[tap:tap + 1, :]
        head = cur if head is None else head + cur
    return jnp.concatenate([head, acc[HALO:rows]], axis=0)


def _last_rows(x_ref):
    rows = x_ref.shape[0]
    return x_ref[rows - 2 * HALO:rows, :].astype(F32)[HALO:2 * HALO]


def _conv_packed(u, hist, cw_ref, cols, seq_len):
    rows = u.shape[0]
    pos = lax.broadcasted_iota(jnp.int32, (rows, 1), 0) & (seq_len - 1)
    acc = None
    for tap in range(CONV_W):
        back = CONV_W - 1 - tap
        cur = u if back == 0 else pltpu.roll(u, back, 0)
        if tap < CONV_W - 1:
            old = hist if tap == 0 else pltpu.roll(hist, rows - tap, 0)
            cur = jnp.where(pos + tap <= CONV_W - 2, old, cur)
        term = cur * cw_ref[tap:tap + 1, cols]
        acc = term if acc is None else acc + term
    return acc


def _last_of_seq(x, seq_len):
    rows = x.shape[0]
    if seq_len == rows:
        return jnp.broadcast_to(x[rows - 1:rows, :], x.shape)
    pos = lax.broadcasted_iota(jnp.int32, x.shape, 0) & (seq_len - 1)
    out = x
    for ahead in range(1, seq_len):
        out = jnp.where(pos == seq_len - 1 - ahead, pltpu.roll(x, rows - ahead, 0), out)
    return out


def _pad_lanes(v, offset=0):
    return jnp.pad(v.astype(F32), (offset, LANES - offset - v.shape[0])).reshape(1, LANES)


def _packed_history(buf, seq_len):
    b, r, c = buf.shape
    return jnp.pad(buf, ((0, 0), (0, seq_len - r), (0, 0))).reshape(b * seq_len, c)


def _ssd_expand(x, g):
    rows = x.shape[0]
    head = lax.broadcasted_iota(jnp.int32, (rows, SSD_GROUP_INNER), 1) >> _log2(SSD_HEADDIM)
    h0 = g * SSD_GROUP_HEADS
    out = jnp.broadcast_to(x[:, h0:h0 + 1], (rows, SSD_GROUP_INNER))
    for r in range(1, SSD_GROUP_HEADS):
        out = jnp.where(head == r, jnp.broadcast_to(x[:, h0 + r:h0 + r + 1], (rows, SSD_GROUP_INNER)), out)
    return out


def _ssd_step_terms(dt_ref, dtb_ref, alog_ref, seq_len):
    dt = jax.nn.softplus(dt_ref[...] + dtb_ref[...])
    la = dt * (-jnp.exp(alog_ref[...]))
    ct = _cumsum_rows(la, _seq_masks(la.shape[0], seq_len)[0])
    return dt, ct


def _ssd_group_intra(g, xs, bg, cg, dt, ct, ct_t, incl):
    rows = xs.shape[0]
    scores = _dot_nt(cg, bg)
    xdt = xs * _ssd_expand(dt, g)
    head = lax.broadcasted_iota(jnp.int32, (rows, SSD_GROUP_INNER), 1) >> _log2(SSD_HEADDIM)
    y = jnp.zeros((rows, SSD_GROUP_INNER), F32)
    for r in range(SSD_GROUP_HEADS):
        h = g * SSD_GROUP_HEADS + r
        diff = ct[:, h:h + 1] - ct_t[h:h + 1, :]
        d = jnp.where(incl, jnp.exp(jnp.where(incl, diff, 0.0)), 0.0)
        a = (scores * d).astype(BF16)
        y = y + _dot(a, jnp.where(head == r, xdt, 0.0).astype(BF16))
    return y, xdt, _ssd_expand(ct, g)


def _ssd_gate_norm(y, z, nw):
    return _rms_rows(y * jax.nn.silu(z.astype(F32))) * nw


def _ssd_prompt_kernel(z_ref, x_ref, bc_ref, dt_ref, cwx_ref, cwbc_ref, cbx_ref, cbbc_ref, dtb_ref, alog_ref,
                       dsk_ref, nw_ref, y_ref, so_ref, px_ref, pbc_ref, st_ref):
    c = pl.program_id(1)
    rows = x_ref.shape[0]
    gs = SSD_GROUPS * SSD_STATE

    @pl.when(c == 0)
    def _():
        px_ref[...] = jnp.zeros(px_ref.shape, F32)
        pbc_ref[...] = jnp.zeros(pbc_ref.shape, F32)
        st_ref[...] = jnp.zeros(st_ref.shape, F32)

    dt, ct = _ssd_step_terms(dt_ref, dtb_ref, alog_ref, rows)
    ct_t = ct.T
    incl, _ = _seq_masks(rows, rows)
    xs_all = jax.nn.silu(_conv_chunk_mxu(x_ref[...], px_ref[...], cwx_ref) + cbx_ref[...])
    bc_all = jax.nn.silu(_conv_chunk_mxu(bc_ref[...], pbc_ref[...], cwbc_ref) + cbbc_ref[...]).astype(BF16)

    for g in range(SSD_GROUPS):
        ci = slice(g * SSD_GROUP_INNER, (g + 1) * SSD_GROUP_INNER)
        xs = xs_all[:, ci]
        bg = bc_all[:, g * SSD_STATE:(g + 1) * SSD_STATE]
        cg = bc_all[:, gs + g * SSD_STATE:gs + (g + 1) * SSD_STATE]
        y, xdt, ctc = _ssd_group_intra(g, xs, bg, cg, dt, ct, ct_t, incl)
        st = st_ref[g]
        y = y + jnp.exp(ctc) * _dot(cg, st.astype(BF16)) + dsk_ref[:, ci] * xs
        last = ctc[rows - 1:rows, :]
        st_ref[g] = st * jnp.exp(last) + _dot_tn(bg, (xdt * jnp.exp(last - ctc)).astype(BF16))
        y_ref[:, ci] = _ssd_gate_norm(y, z_ref[:, ci], nw_ref[:, ci]).astype(y_ref.dtype)

    px_ref[...] = _last_rows(x_ref)
    pbc_ref[...] = _last_rows(bc_ref)

    @pl.when(c == pl.num_programs(1) - 1)
    def _():
        for g in range(SSD_GROUPS):
            hs = slice(g * SSD_GROUP_HEADS, (g + 1) * SSD_GROUP_HEADS)
            so_ref[0, hs] = st_ref[g].T.reshape(SSD_GROUP_HEADS, SSD_HEADDIM, SSD_STATE)


def _ssd_prompt(proj, dtp, prm, batch, seq):
    t = min(SSD_CHUNK, seq)
    nc = seq // t
    row = lambda b, c: b * nc + c
    half = SSD_INNER
    full = lambda shape: pl.BlockSpec(shape, lambda b, c: (0,) * len(shape))
    return pl.pallas_call(
        _ssd_prompt_kernel,
        grid=(batch, nc),
        in_specs=[
            pl.BlockSpec((t, half), lambda b, c: (row(b, c), 0)),
            pl.BlockSpec((t, half), lambda b, c: (row(b, c), 1)),
            pl.BlockSpec((t, half), lambda b, c: (row(b, c), 2)),
            pl.BlockSpec((t, LANES), lambda b, c: (row(b, c), 0)),
            pl.BlockSpec((CONV_W, half), lambda b, c: (0, 0)),
            pl.BlockSpec((CONV_W, half), lambda b, c: (0, 1)),
            pl.BlockSpec((1, half), lambda b, c: (0, 0)),
            pl.BlockSpec((1, half), lambda b, c: (0, 1)),
            full((1, LANES)), full((1, LANES)), full((1, half)), full((1, half)),
        ],
        out_specs=[
            pl.BlockSpec((t, half), lambda b, c: (row(b, c), 0)),
            pl.BlockSpec((1, SSD_HEADS, SSD_HEADDIM, SSD_STATE), lambda b, c: (b, 0, 0, 0)),
        ],
        out_shape=[
            jax.ShapeDtypeStruct((batch * seq, half), BF16),
            jax.ShapeDtypeStruct((batch, SSD_HEADS, SSD_HEADDIM, SSD_STATE), F32),
        ],
        scratch_shapes=[
            pltpu.VMEM((HALO, half), F32),
            pltpu.VMEM((HALO, half), F32),
            pltpu.VMEM((SSD_GROUPS, SSD_STATE, SSD_GROUP_INNER), F32),
        ],
        compiler_params=_params("parallel", "arbitrary"),
        name="ssd_prompt",
    )(proj, proj, proj, dtp, prm["cw"], prm["cw"], prm["cb"], prm["cb"], prm["dtb"], prm["alog"],
      prm["dsk"], prm["nw"])


SSD_SAMPLE_SEQS = 4


def _ssd_sample_kernel(z_ref, x_ref, bc_ref, hx_ref, hbc_ref, dt_ref, cwx_ref, cwbc_ref, cbx_ref, cbbc_ref,
                       dtb_ref, alog_ref, dsk_ref, nw_ref, s0_ref, y_ref, so_ref,
                       acc_ref, ect_ref, xdw_ref, b_ref, c_ref, ct_ref, *, seq_len):
    j = pl.program_id(1)
    rows = x_ref.shape[0]
    gs = SSD_GROUPS * SSD_STATE

    @pl.when(j == 0)
    def _():
        dt, ct = _ssd_step_terms(dt_ref, dtb_ref, alog_ref, seq_len)
        ct_ref[...] = ct
        ct_t = ct.T
        incl, _ = _seq_masks(rows, seq_len)
        for g in range(SSD_GROUPS):
            ci = slice(g * SSD_GROUP_INNER, (g + 1) * SSD_GROUP_INNER)
            cb_ = slice(g * SSD_STATE, (g + 1) * SSD_STATE)
            cc_ = slice(gs + g * SSD_STATE, gs + (g + 1) * SSD_STATE)
            xs = jax.nn.silu(_conv_packed(x_ref[:, ci], hx_ref[:, ci], cwx_ref, ci, seq_len) + cbx_ref[:, ci])
            bg = jax.nn.silu(_conv_packed(bc_ref[:, cb_], hbc_ref[:, cb_], cwbc_ref, cb_, seq_len)
                             + cbbc_ref[:, cb_]).astype(BF16)
            cg = jax.nn.silu(_conv_packed(bc_ref[:, cc_], hbc_ref[:, cc_], cwbc_ref, cc_, seq_len)
                             + cbbc_ref[:, cc_]).astype(BF16)
            y, xdt, ctc = _ssd_group_intra(g, xs, bg, cg, dt, ct, ct_t, incl)
            acc_ref[:, ci] = y + dsk_ref[:, ci] * xs
            ect_ref[:, ci] = jnp.exp(ctc)
            xdw_ref[:, ci] = xdt * jnp.exp(_last_of_seq(ctc, seq_len) - ctc)
            b_ref[:, cb_] = bg
            c_ref[:, cb_] = cg

    row_seq = lax.broadcasted_iota(jnp.int32, (rows, 1), 0) >> _log2(seq_len)
    for n in range(SSD_SAMPLE_SEQS):
        seq = j * SSD_SAMPLE_SEQS + n
        mine = row_seq == seq
        last_row = seq * seq_len + (seq_len - 1)
        for g in range(SSD_GROUPS):
            ci = slice(g * SSD_GROUP_INNER, (g + 1) * SSD_GROUP_INNER)
            cb_ = slice(g * SSD_STATE, (g + 1) * SSD_STATE)
            hs = slice(g * SSD_GROUP_HEADS, (g + 1) * SSD_GROUP_HEADS)
            sg = s0_ref[n, hs].reshape(SSD_GROUP_INNER, SSD_STATE)
            y_state = ect_ref[:, ci] * _dot_nt(c_ref[:, cb_], sg.astype(BF16))
            acc_ref[:, ci] += jnp.where(mine, y_state, 0.0)
            upd = _dot_tn(jnp.where(mine, xdw_ref[:, ci], 0.0).astype(BF16), b_ref[:, cb_])
            for r in range(SSD_GROUP_HEADS):
                h = g * SSD_GROUP_HEADS + r
                keep = jnp.exp(ct_ref[pl.ds(last_row, 1), pl.ds(h, 1)])
                so_ref[n, h] = s0_ref[n, h] * keep + upd[r * SSD_HEADDIM:(r + 1) * SSD_HEADDIM, :]

    @pl.when(j == pl.num_programs(1) - 1)
    def _():
        for g in range(SSD_GROUPS):
            ci = slice(g * SSD_GROUP_INNER, (g + 1) * SSD_GROUP_INNER)
            y_ref[:, ci] = _ssd_gate_norm(acc_ref[:, ci], z_ref[:, ci], nw_ref[:, ci]).astype(y_ref.dtype)


def _ssd_sample(proj, dtp, hist, prm, s0, seq_len):
    m = proj.shape[0]
    rows = SAMPLE_TILE
    nb = SSD_SAMPLE_SEQS
    steps = rows // seq_len // nb
    half = SSD_INNER
    full = lambda shape: pl.BlockSpec(shape, lambda i, j: (0,) * len(shape))
    state_spec = pl.BlockSpec((nb, SSD_HEADS, SSD_HEADDIM, SSD_STATE), lambda i, j: (i * steps + j, 0, 0, 0))
    return pl.pallas_call(
        functools.partial(_ssd_sample_kernel, seq_len=seq_len),
        grid=(m // rows, steps),
        in_specs=[
            pl.BlockSpec((rows, half), lambda i, j: (i, 0)),
            pl.BlockSpec((rows, half), lambda i, j: (i, 1)),
            pl.BlockSpec((rows, half), lambda i, j: (i, 2)),
            pl.BlockSpec((rows, half), lambda i, j: (i, 0)),
            pl.BlockSpec((rows, half), lambda i, j: (i, 1)),
            pl.BlockSpec((rows, LANES), lambda i, j: (i, 0)),
            pl.BlockSpec((CONV_W, half), lambda i, j: (0, 0)),
            pl.BlockSpec((CONV_W, half), lambda i, j: (0, 1)),
            pl.BlockSpec((1, half), lambda i, j: (0, 0)),
            pl.BlockSpec((1, half), lambda i, j: (0, 1)),
            full((1, LANES)), full((1, LANES)), full((1, half)), full((1, half)),
            state_spec,
        ],
        out_specs=[pl.BlockSpec((rows, half), lambda i, j: (i, 0)), state_spec],
        out_shape=[
            jax.ShapeDtypeStruct((m, half), BF16),
            jax.ShapeDtypeStruct(s0.shape, F32),
        ],
        scratch_shapes=[
            pltpu.VMEM((rows, half), F32),
            pltpu.VMEM((rows, half), F32),
            pltpu.VMEM((rows, half), F32),
            pltpu.VMEM((rows, SSD_GROUPS * SSD_STATE), BF16),
            pltpu.VMEM((rows, SSD_GROUPS * SSD_STATE), BF16),
            pltpu.VMEM((rows, LANES), F32),
        ],
        compiler_params=_params("parallel", "arbitrary"),
        name="ssd_sample",
    )(proj, proj, proj, hist, hist, dtp, prm["cw"], prm["cw"], prm["cb"], prm["cb"], prm["dtb"], prm["alog"],
      prm["dsk"], prm["nw"], s0)


def _ssd_layer(x, s0, cbuf, norm_w, w_main, w_dt, prm, *, batch, seq, sample):
    proj = _rms_proj(x, norm_w, w_main, out_dtype=F32 if sample else BF16, tn=1024)
    dtp = _rms_proj(x, norm_w, w_dt, out_dtype=F32, tn=LANES)
    conv_new = proj.reshape(batch, seq, -1)[:, seq - (CONV_W - 1):, SSD_INNER:].astype(F32)
    if sample:
        y, s_new = _ssd_sample(proj, dtp, _packed_history(cbuf, seq), prm, s0, seq)
    else:
        y, s_new = _ssd_prompt(proj, dtp, prm, batch, seq)
    return y, s_new, conv_new


def _l2norm_rows(x):
    return x * lax.rsqrt(jnp.sum(x * x, axis=-1, keepdims=True) + RMS_EPS)


def _gdn_gates(ba_ref, dtb_ref, alog_ref, seq_len):
    ba = ba_ref[...]
    beta = jax.nn.sigmoid(ba)
    g = -jnp.exp(alog_ref[...]) * jax.nn.softplus(ba + dtb_ref[...])
    gc = _cumsum_rows(g, _seq_masks(g.shape[0], seq_len)[0])
    return beta, gc, _last_of_seq(gc, seq_len)


INV_BASE_BLOCK = 16
GDN_PROMPT_KEY_HEADS = 4


def _unit_lower_inverse(ls, seq_len):
    rows = ls[0].shape[0]
    ti = lax.broadcasted_iota(jnp.int32, (rows, rows), 0)
    si = lax.broadcasted_iota(jnp.int32, (rows, rows), 1)
    base = min(INV_BASE_BLOCK, seq_len)
    diag_block = (ti >> _log2(base)) == (si >> _log2(base))
    ns = [jnp.where(diag_block, -l, 0.0) for l in ls]
    eye = jnp.where(ti == si, 1.0, 0.0)
    xs = [eye + n for n in ns]
    for _ in range(_log2(base) - 1):
        n16s = [n.astype(BF16) for n in ns]
        ns = [_dot(n16, n16) for n16 in n16s]
        xs = [x + _dot(x.astype(BF16), n.astype(BF16)) for x, n in zip(xs, ns)]
    blk = base
    while blk < seq_len:
        lower_left = ((ti >> _log2(2 * blk)) == (si >> _log2(2 * blk))) & ((ti >> _log2(blk)) != (si >> _log2(blk)))
        cs = [jnp.where(lower_left, l, 0.0).astype(BF16) for l in ls]
        x16s = [x.astype(BF16) for x in xs]
        ts = [_dot(x16, c).astype(BF16) for x16, c in zip(x16s, cs)]
        xs = [x - _dot(t, x16) for x, t, x16 in zip(xs, ts, x16s)]
        blk *= 2
    return xs


def _gdn_heads_intra(qs, ks, kks, qks, vs, beta_all, gc_all, gl_all, gc_t, heads, incl, strict, seq_len):
    rows = qs[0].shape[0]
    betas, gccs, gcls, ds = [], [], [], []
    for head in heads:
        if isinstance(head, int):
            beta = beta_all[:, head:head + 1]
            gcc = gc_all[:, GDN_VH + head:GDN_VH + head + 1]
            gcl = gl_all[:, GDN_VH + head:GDN_VH + head + 1]
            gcr = gc_t[GDN_VH + head:GDN_VH + head + 1, :]
        else:
            lane = lax.broadcasted_iota(jnp.int32, (rows, LANES), 1)
            sub = lax.broadcasted_iota(jnp.int32, (LANES, rows), 0)
            beta = jnp.sum(jnp.where(lane == head, beta_all, 0.0), axis=1, keepdims=True)
            gcc = jnp.sum(jnp.where(lane == GDN_VH + head, gc_all, 0.0), axis=1, keepdims=True)
            gcl = jnp.sum(jnp.where(lane == GDN_VH + head, gl_all, 0.0), axis=1, keepdims=True)
            gcr = jnp.sum(jnp.where(sub == GDN_VH + head, gc_t, 0.0), axis=0, keepdims=True)
        betas.append(beta)
        gccs.append(gcc)
        gcls.append(gcl)
        ds.append(jnp.where(incl, jnp.exp(jnp.where(incl, gcc - gcr, 0.0)), 0.0))
    key = lambda i: i // GDN_REP
    n = len(heads)
    ps = _unit_lower_inverse(
        [jnp.where(strict, (betas[i] * kks[key(i)]) * ds[i], 0.0) for i in range(n)], seq_len)
    egcs = [jnp.exp(g) for g in gccs]
    rhss = [jnp.concatenate([vs[i] * betas[i], ks[key(i)] * (betas[i] * egcs[i])], axis=1).astype(BF16)
            for i in range(n)]
    sols = [_dot(ps[i].astype(BF16), rhss[i]) for i in range(n)]
    us = [s[:, :GDN_DV] for s in sols]
    ws = [s[:, GDN_DV:].astype(BF16) for s in sols]
    attns = [(qks[key(i)] * ds[i]).astype(BF16) for i in range(n)]
    qds = [(qs[key(i)] * egcs[i]).astype(BF16) for i in range(n)]
    kds = [(ks[key(i)] * jnp.exp(gcls[i] - gccs[i])).astype(BF16) for i in range(n)]
    return us, ws, attns, qds, kds, gccs


def _gdn_gate_norm(o, z, nw):
    return _rms_rows(o) * nw * jax.nn.silu(z.astype(F32))


def _gdn_prompt_kernel(q_ref, k_ref, v_ref, z_ref, ba_ref, cwq_ref, cwk_ref, cwv_ref, dtb_ref, alog_ref, nw_ref,
                       o_ref, so_ref, pq_ref, pk_ref, pv_ref, s_ref):
    c = pl.program_id(1)
    rows = q_ref.shape[0]

    @pl.when(c == 0)
    def _():
        for prev in (pq_ref, pk_ref, pv_ref):
            prev[...] = jnp.zeros(prev.shape, F32)
        s_ref[...] = jnp.zeros(s_ref.shape, F32)

    beta_all, gc_all, gl_all = _gdn_gates(ba_ref, dtb_ref, alog_ref, rows)
    gc_t = gc_all.T
    incl, strict = _seq_masks(rows, rows)

    for kh0 in range(0, GDN_KH, GDN_PROMPT_KEY_HEADS):
        khs = range(kh0, kh0 + GDN_PROMPT_KEY_HEADS)
        heads = [kh * GDN_REP + r for kh in khs for r in range(GDN_REP)]
        key_cols = [slice(kh * GDN_DK, (kh + 1) * GDN_DK) for kh in khs]
        val_cols = [slice(h * GDN_DV, (h + 1) * GDN_DV) for h in heads]
        qs = [_l2norm_rows(jax.nn.silu(_conv_chunk(q_ref[:, c_].astype(F32), pq_ref[:, c_], cwq_ref, c_)))
              * (GDN_DK ** -0.5) for c_ in key_cols]
        ks = [_l2norm_rows(jax.nn.silu(_conv_chunk(k_ref[:, c_].astype(F32), pk_ref[:, c_], cwk_ref, c_)))
              for c_ in key_cols]
        vs = [jax.nn.silu(_conv_chunk(v_ref[:, c_].astype(F32), pv_ref[:, c_], cwv_ref, c_)) for c_ in val_cols]
        q16s = [q.astype(BF16) for q in qs]
        k16s = [k.astype(BF16) for k in ks]
        kks = [_dot_nt(k16, k16) for k16 in k16s]
        qks = [_dot_nt(q16, k16) for q16, k16 in zip(q16s, k16s)]
        us, ws, attns, qds, kds, gccs = _gdn_heads_intra(
            qs, ks, kks, qks, vs, beta_all, gc_all, gl_all, gc_t, heads, incl, strict, rows)
        n = len(heads)
        s_olds = [s_ref[h] for h in heads]
        s16s = [s.astype(BF16) for s in s_olds]
        v_news = [us[i] - _dot(ws[i], s16s[i]) for i in range(n)]
        vn16s = [v.astype(BF16) for v in v_news]
        os_ = [_dot(qds[i], s16s[i]) + _dot(attns[i], vn16s[i]) for i in range(n)]
        for i, h in enumerate(heads):
            s_ref[h] = s_olds[i] * jnp.exp(gccs[i][rows - 1:rows, :]) + _dot_tn(kds[i], vn16s[i])
            o_ref[:, val_cols[i]] = _gdn_gate_norm(os_[i], z_ref[:, val_cols[i]], nw_ref[...]).astype(o_ref.dtype)

    pq_ref[...] = _last_rows(q_ref)
    pk_ref[...] = _last_rows(k_ref)
    pv_ref[...] = _last_rows(v_ref)

    @pl.when(c == pl.num_programs(1) - 1)
    def _():
        so_ref[0] = s_ref[...]


def _gdn_col_specs(rows, row_map):
    kcol = GDN_KD // GDN_DK
    vcol = 2 * GDN_KD // (GDN_REP * GDN_DV)
    return [
        pl.BlockSpec((rows, GDN_DK), lambda *ix: (row_map(*ix), ix[1])),
        pl.BlockSpec((rows, GDN_DK), lambda *ix: (row_map(*ix), kcol + ix[1])),
        pl.BlockSpec((rows, GDN_REP * GDN_DV), lambda *ix: (row_map(*ix), vcol + ix[1])),
    ]


def _gdn_prompt(proj, ba, prm, batch, seq):
    t = min(GDN_CHUNK, seq)
    nc = seq // t
    row = lambda b, c: b * nc + c
    full = lambda shape: pl.BlockSpec(shape, lambda b, c: (0,) * len(shape))
    return pl.pallas_call(
        _gdn_prompt_kernel,
        grid=(batch, nc),
        in_specs=[
            pl.BlockSpec((t, GDN_KD), lambda b, c: (row(b, c), 0)),
            pl.BlockSpec((t, GDN_KD), lambda b, c: (row(b, c), 1)),
            pl.BlockSpec((t, GDN_V), lambda b, c: (row(b, c), 1)),
            pl.BlockSpec((t, GDN_V), lambda b, c: (row(b, c), 2)),
            pl.BlockSpec((t, LANES), lambda b, c: (row(b, c), 0)),
            pl.BlockSpec((CONV_W, GDN_KD), lambda b, c: (0, 0)),
            pl.BlockSpec((CONV_W, GDN_KD), lambda b, c: (0, 1)),
            pl.BlockSpec((CONV_W, GDN_V), lambda b, c: (0, 1)),
            full((1, LANES)), full((1, LANES)), full((1, GDN_DV)),
        ],
        out_specs=[
            pl.BlockSpec((t, GDN_V), lambda b, c: (row(b, c), 0)),
            pl.BlockSpec((1, GDN_VH, GDN_DK, GDN_DV), lambda b, c: (b, 0, 0, 0)),
        ],
        out_shape=[
            jax.ShapeDtypeStruct((batch * seq, GDN_V), BF16),
            jax.ShapeDtypeStruct((batch, GDN_VH, GDN_DK, GDN_DV), F32),
        ],
        scratch_shapes=[
            pltpu.VMEM((HALO, GDN_KD), F32),
            pltpu.VMEM((HALO, GDN_KD), F32),
            pltpu.VMEM((HALO, GDN_V), F32),
            pltpu.VMEM((GDN_VH, GDN_DK, GDN_DV), F32),
        ],
        compiler_params=_params("parallel", "arbitrary"),
        name="gdn_prompt",
    )(proj, proj, proj, proj, ba, prm["cw"], prm["cw"], prm["cw"], prm["dtb"], prm["alog"], prm["nw"])


GDN_SAMPLE_SEQS = 8


def _gdn_sample_kernel(q_ref, k_ref, v_ref, z_ref, ba_ref, hq_ref, hk_ref, hv_ref, cwq_ref, cwk_ref, cwv_ref,
                       dtb_ref, alog_ref, nw_ref, s0_ref, o_ref, so_ref,
                       u_ref, w_ref, attn_ref, qd_ref, kd_ref, gcc_ref, vnew_ref, qs_ref, *, seq_len):
    kh = pl.program_id(1)
    j = pl.program_id(2)
    rows = q_ref.shape[0]

    @pl.when(j == 0)
    def _():
        one = slice(0, GDN_DK)
        q = _l2norm_rows(jax.nn.silu(_conv_packed(q_ref[...], hq_ref[...], cwq_ref, one, seq_len))) * (GDN_DK ** -0.5)
        k = _l2norm_rows(jax.nn.silu(_conv_packed(k_ref[...], hk_ref[...], cwk_ref, one, seq_len)))
        v_all = jax.nn.silu(_conv_packed(v_ref[...], hv_ref[...], cwv_ref, slice(0, GDN_REP * GDN_DV), seq_len))
        beta_all, gc_all, gl_all = _gdn_gates(ba_ref, dtb_ref, alog_ref, seq_len)
        gc_t = gc_all.T
        incl, strict = _seq_masks(rows, seq_len)
        q16, k16 = q.astype(BF16), k.astype(BF16)
        kk = _dot_nt(k16, k16)
        qk = _dot_nt(q16, k16)
        vs = [v_all[:, r * GDN_DV:(r + 1) * GDN_DV] for r in range(GDN_REP)]
        heads = [kh * GDN_REP + r for r in range(GDN_REP)]
        us, ws, attns, qds, kds, gccs = _gdn_heads_intra(
            [q], [k], [kk], [qk], vs, beta_all, gc_all, gl_all, gc_t, heads, incl, strict, seq_len)
        for r in range(GDN_REP):
            u_ref[r] = us[r]
            w_ref[r] = ws[r]
            attn_ref[r] = attns[r]
            qd_ref[r] = qds[r]
            kd_ref[r] = kds[r]
            gcc_ref[r] = jnp.broadcast_to(gccs[r], (rows, LANES))

    win = GDN_SAMPLE_SEQS * seq_len
    row0 = pl.multiple_of(j * win, win)
    in_win = pl.ds(row0, win)
    win_seq = lax.broadcasted_iota(jnp.int32, (win, GDN_DV), 0) >> _log2(seq_len)
    for r in range(GDN_REP):
        u_w, w_w, qd_w = u_ref[r, in_win, :], w_ref[r, in_win, :], qd_ref[r, in_win, :]
        kd_w = kd_ref[r, in_win, :].astype(F32)
        v_new_all = jnp.zeros((win, GDN_DV), F32)
        qs_all = jnp.zeros((win, GDN_DV), F32)
        for n in range(GDN_SAMPLE_SEQS):
            mine = win_seq == n
            s_old = s0_ref[n, r]
            s16 = s_old.astype(BF16)
            v_new = u_w - _dot(w_w, s16)
            v_new_all = jnp.where(mine, v_new, v_new_all)
            qs_all = jnp.where(mine, _dot(qd_w, s16), qs_all)
            kd = jnp.where(mine, kd_w, 0.0).astype(BF16)
            keep = jnp.exp(gcc_ref[r, pl.ds(row0 + n * seq_len + (seq_len - 1), 1), :])
            so_ref[n, r] = s_old * keep + _dot_tn(kd, v_new.astype(BF16))
        vnew_ref[r, in_win, :] = v_new_all
        qs_ref[r, in_win, :] = qs_all

    @pl.when(j == pl.num_programs(2) - 1)
    def _():
        for r in range(GDN_REP):
            cv = slice(r * GDN_DV, (r + 1) * GDN_DV)
            o = qs_ref[r] + _dot(attn_ref[r], vnew_ref[r].astype(BF16))
            o_ref[:, cv] = _gdn_gate_norm(o, z_ref[:, cv], nw_ref[...]).astype(o_ref.dtype)


def _gdn_sample(proj, ba, hist, prm, s0, seq_len):
    m = proj.shape[0]
    rows = SAMPLE_TILE
    nb = GDN_SAMPLE_SEQS
    steps = rows // seq_len // nb
    row_map = lambda i, kh, j: i
    zcol = (2 * GDN_KD + GDN_V) // (GDN_REP * GDN_DV)
    pair = GDN_REP * GDN_DV
    full = lambda shape: pl.BlockSpec(shape, lambda i, kh, j: (0,) * len(shape))
    state_spec = pl.BlockSpec((nb, GDN_REP, GDN_DK, GDN_DV), lambda i, kh, j: (i * steps + j, kh, 0, 0))
    head_f32 = pltpu.VMEM((GDN_REP, rows, GDN_DV), F32)
    head_bf16 = pltpu.VMEM((GDN_REP, rows, GDN_DV), BF16)
    return pl.pallas_call(
        functools.partial(_gdn_sample_kernel, seq_len=seq_len),
        grid=(m // rows, GDN_KH, steps),
        in_specs=_gdn_col_specs(rows, row_map) + [
            pl.BlockSpec((rows, pair), lambda i, kh, j: (i, zcol + kh)),
            pl.BlockSpec((rows, LANES), lambda i, kh, j: (i, 0)),
        ] + _gdn_col_specs(rows, row_map) + _gdn_col_specs(CONV_W, lambda i, kh, j: 0) + [
            full((1, LANES)), full((1, LANES)), full((1, GDN_DV)), state_spec],
        out_specs=[pl.BlockSpec((rows, pair), lambda i, kh, j: (i, kh)), state_spec],
        out_shape=[
            jax.ShapeDtypeStruct((m, GDN_V), BF16),
            jax.ShapeDtypeStruct(s0.shape, F32),
        ],
        scratch_shapes=[head_f32, head_bf16, head_bf16, head_bf16, head_bf16, head_f32, head_f32, head_f32],
        compiler_params=_params("parallel", "parallel", "arbitrary"),
        name="gdn_sample",
    )(proj, proj, proj, proj, ba, hist, hist, hist, prm["cw"], prm["cw"], prm["cw"], prm["dtb"], prm["alog"],
      prm["nw"], s0)


def _gdn_layer(x, s0, cbuf, norm_w, w_main, w_ba, prm, *, batch, seq, sample):
    proj = _rms_proj(x, norm_w, w_main, out_dtype=F32 if sample else BF16, tn=1024)
    ba = _rms_proj(x, norm_w, w_ba, out_dtype=F32, tn=LANES)
    conv_new = proj.reshape(batch, seq, -1)[:, seq - (CONV_W - 1):, :2 * GDN_KD + GDN_V].astype(F32)
    if sample:
        o, s_new = _gdn_sample(proj, ba, _packed_history(cbuf, seq), prm, s0, seq)
    else:
        o, s_new = _gdn_prompt(proj, ba, prm, batch, seq)
    return o, s_new, conv_new


def _layer_of(stack, j):
    if stack is None:
        return None
    return stack.reshape(stack.shape[1:]) if stack.shape[0] == 1 else stack[j]


def _stack(layers):
    return layers[0].reshape((1,) + layers[0].shape) if len(layers) == 1 else jnp.stack(layers)


def _trunk(x, weights, states, *, batch, seq, pos0, sample):
    s_ret, s_ssd, c_ssd, s_gdn, c_gdn = states
    n_ssd, n_ssdc, n_gdn, n_gdnc = [], [], [], []
    ret_stack = None
    depth = weights["norm_mix"].shape[0]
    n_ret_layers = weights["ret_w_in"].shape[0]
    for i in range(depth):
        kind, jdx = i % 3, i // 3
        nm = weights["norm_mix"][i]
        if kind == 0:
            y, ret_stack = _ret_layer(x, s_ret, ret_stack, jdx, n_ret_layers, nm, weights["ret_w_in"][jdx],
                                      batch=batch, seq=seq, pos0=pos0, sample=sample)
            w_out = weights["ret_w_out"][jdx]
        elif kind == 1:
            y, s, cnew = _ssd_layer(x, _layer_of(s_ssd, jdx), _layer_of(c_ssd, jdx), nm, weights["ssd_w_main"][jdx],
                                    weights["ssd_w_dt"][jdx], weights["ssd_prm"][jdx],
                                    batch=batch, seq=seq, sample=sample)
            w_out = weights["ssd_w_out"][jdx]
            n_ssd.append(s)
            n_ssdc.append(cnew)
        else:
            y, s, cnew = _gdn_layer(x, _layer_of(s_gdn, jdx), _layer_of(c_gdn, jdx), nm, weights["gdn_w_main"][jdx],
                                    weights["gdn_w_ba"][jdx], weights["gdn_prm"][jdx],
                                    batch=batch, seq=seq, sample=sample)
            w_out = weights["gdn_w_out"][jdx]
            n_gdn.append(s)
            n_gdnc.append(cnew)
        final_w = weights["norm_final"] if i == depth - 1 else None
        x = _mix_mlp(y, w_out, x, weights["norm_mlp"][i], weights["mlp_w_up"][i], weights["mlp_w_down"][i], final_w)
    return (x.reshape(batch, seq, -1), ret_stack, _stack(n_ssd), _stack(n_ssdc), _stack(n_gdn), _stack(n_gdnc))


def _prepare_weights(norm_mix, norm_mlp, norm_final, ret_w_in, ret_w_out,
                     ssd_w_in, ssd_conv_w, ssd_conv_b, ssd_dt_bias, ssd_a_log, ssd_d, ssd_norm, ssd_w_out,
                     gdn_w_in, gdn_conv_w, gdn_dt_bias, gdn_a_log, gdn_norm, gdn_w_out, mlp_w_up, mlp_w_down):
    n_ssd, n_gdn = ssd_w_in.shape[0], gdn_w_in.shape[0]
    main = 3 * SSD_INNER

    def small(w):
        return jnp.pad(w[:, main:], ((0, 0), (0, LANES - (w.shape[1] - main)))).astype(BF16)

    ssd_prm = [dict(cw=ssd_conv_w[i], cb=ssd_conv_b[i].reshape(1, -1), dtb=_pad_lanes(ssd_dt_bias[i]),
                    alog=_pad_lanes(ssd_a_log[i]), dsk=jnp.repeat(ssd_d[i], SSD_HEADDIM).reshape(1, -1),
                    nw=ssd_norm[i].reshape(1, -1)) for i in range(n_ssd)]
    gdn_prm = [dict(cw=gdn_conv_w[i], dtb=_pad_lanes(gdn_dt_bias[i], GDN_VH), alog=_pad_lanes(gdn_a_log[i], GDN_VH),
                    nw=gdn_norm[i].reshape(1, -1)) for i in range(n_gdn)]
    return dict(
        norm_mix=norm_mix, norm_mlp=norm_mlp, norm_final=norm_final,
        ret_w_in=ret_w_in.astype(BF16), ret_w_out=ret_w_out.astype(BF16),
        ssd_w_main=[ssd_w_in[i, :, :main].astype(BF16) for i in range(n_ssd)],
        ssd_w_dt=[small(ssd_w_in[i]) for i in range(n_ssd)],
        ssd_prm=ssd_prm, ssd_w_out=ssd_w_out.astype(BF16),
        gdn_w_main=[gdn_w_in[i, :, :main].astype(BF16) for i in range(n_gdn)],
        gdn_w_ba=[small(gdn_w_in[i]) for i in range(n_gdn)],
        gdn_prm=gdn_prm, gdn_w_out=gdn_w_out.astype(BF16),
        mlp_w_up=mlp_w_up.astype(BF16), mlp_w_down=mlp_w_down.astype(BF16),
    )


def kernel(x_prompt, x_sample, state_ret, state_ssd, state_ssd_conv, state_gdn, state_gdn_conv, norm_mix, norm_mlp,
           norm_final, ret_w_in, ret_w_out, ssd_w_in, ssd_conv_w, ssd_conv_b, ssd_dt_bias, ssd_a_log, ssd_d, ssd_norm,
           ssd_w_out, gdn_w_in, gdn_conv_w, gdn_dt_bias, gdn_a_log, gdn_norm, gdn_w_out, mlp_w_up, mlp_w_down):
    weights = _prepare_weights(norm_mix, norm_mlp, norm_final, ret_w_in, ret_w_out, ssd_w_in, ssd_conv_w, ssd_conv_b,
                               ssd_dt_bias, ssd_a_log, ssd_d, ssd_norm, ssd_w_out, gdn_w_in, gdn_conv_w, gdn_dt_bias,
                               gdn_a_log, gdn_norm, gdn_w_out, mlp_w_up, mlp_w_down)
    bp, lp, d = x_prompt.shape
    bs, ls, _ = x_sample.shape
    assert lp % RET_CHUNK == 0 and lp % SSD_CHUNK == 0 and lp % GDN_CHUNK == 0
    assert SAMPLE_TILE % ls == 0 and (bs * ls) % SAMPLE_TILE == 0 and ls >= CONV_W - 1
    prompt = _trunk(x_prompt.reshape(bp * lp, d), weights, (None,) * 5, batch=bp, seq=lp, pos0=0, sample=False)
    sample = _trunk(x_sample.reshape(bs * ls, d), weights,
                    (state_ret, state_ssd, state_ssd_conv, state_gdn, state_gdn_conv),
                    batch=bs, seq=ls, pos0=PAST_LEN, sample=True)
    return (prompt[0], sample[0]) + prompt[1:] + sample[1:]
```

```python
import functools
import math

import jax
import jax.numpy as jnp
from jax import lax
from jax.experimental import pallas as pl
from jax.experimental.pallas import tpu as pltpu

F32 = jnp.float32
BF16 = jnp.bfloat16

D_MODEL = 1024
RET_HEADS = 4
RET_DK = 256
RET_DV = 512
RET_QK = RET_HEADS * RET_DK
RET_V = RET_HEADS * RET_DV
ROPE_BASE = 10000.0
SSD_INNER = 2048
SSD_HEADDIM = 64
SSD_HEADS = 32
SSD_GROUPS = 8
SSD_STATE = 128
SSD_GROUP_HEADS = SSD_HEADS // SSD_GROUPS
SSD_GROUP_INNER = SSD_INNER // SSD_GROUPS
GDN_KH = 8
GDN_VH = 16
GDN_DK = 128
GDN_DV = 128
GDN_KD = GDN_KH * GDN_DK
GDN_V = GDN_VH * GDN_DV
GDN_REP = GDN_VH // GDN_KH
CONV_W = 4
D_FF = 4096
RMS_EPS = 1e-6
PAST_LEN = 16384

LANES = 128
SUBLANES = 8
VMEM_LIMIT_BYTES = 56 * 1024 * 1024

SAMPLE_TILE = 128
RET_CHUNK = 256
SSD_CHUNK = 128
GDN_CHUNK = 128
HALO = SUBLANES

_NT = (((1,), (1,)), ((), ()))
_TN = (((0,), (0,)), ((), ()))


def _params(*sem):
    return pltpu.CompilerParams(dimension_semantics=sem, vmem_limit_bytes=VMEM_LIMIT_BYTES)


def _dot(a, b):
    return jnp.dot(a, b, preferred_element_type=F32)


def _dot_nt(a, b):
    return lax.dot_general(a, b, _NT, preferred_element_type=F32)


def _dot_tn(a, b):
    return lax.dot_general(a, b, _TN, preferred_element_type=F32)


def _rms_rows(x):
    return x * lax.rsqrt(jnp.mean(x * x, axis=-1, keepdims=True) + RMS_EPS)


def _log2(n):
    k = n.bit_length() - 1
    assert 1 << k == n, n
    return k


def _seq_masks(rows, seq_len):
    t = lax.broadcasted_iota(jnp.int32, (rows, rows), 0)
    s = lax.broadcasted_iota(jnp.int32, (rows, rows), 1)
    sh = _log2(seq_len)
    same = (t >> sh) == (s >> sh)
    return same & (t >= s), same & (t > s)


def _cumsum_rows(x, incl):
    hi = x.astype(BF16)
    r1 = x - hi.astype(F32)
    mid = r1.astype(BF16)
    lo = (r1 - mid.astype(F32)).astype(BF16)
    tri = jnp.where(incl, 1.0, 0.0).astype(BF16)
    parts = _dot(tri, jnp.concatenate([hi, mid, lo], axis=1))
    w = x.shape[1]
    return (parts[:, 2 * w:] + parts[:, w:2 * w]) + parts[:, :w]


DENSE_ROWS = 512


def _resident(shape):
    return pl.BlockSpec(shape, lambda i: (0,) * len(shape), pipeline_mode=pl.Buffered(1))


def _proj_kernel(x_ref, nw_ref, w_ref, cos_ref, sin_ref, *rest, tn, n_rot, k_scale):
    h = (_rms_rows(x_ref[...]) * nw_ref[...]).astype(BF16)
    if len(rest) == 3:
        ws_ref, o_ref, os_ref = rest
        os_ref[...] = _dot(h, ws_ref[...])
    else:
        (o_ref,) = rest
    for j in range(w_ref.shape[1] // tn):
        cols = slice(j * tn, (j + 1) * tn)
        acc = _dot(h, w_ref[:, cols])
        if j >= n_rot:
            o_ref[:, cols] = acc.astype(o_ref.dtype)
            continue
        c, s = cos_ref[...], sin_ref[...]
        scale = 1.0 if j == 0 else k_scale
        half = RET_DK // 2
        for hd in range(tn // RET_DK):
            lo = j * tn + hd * RET_DK
            x1 = acc[:, hd * RET_DK:hd * RET_DK + half]
            x2 = acc[:, hd * RET_DK + half:(hd + 1) * RET_DK]
            o_ref[:, lo:lo + half] = ((x1 * c - x2 * s) * scale).astype(o_ref.dtype)
            o_ref[:, lo + half:lo + RET_DK] = ((x1 * s + x2 * c) * scale).astype(o_ref.dtype)


def _rms_proj(x, norm_w, w_bf16, w_small, *, out_dtype, tn, rot=None):
    m, d = x.shape
    n = w_bf16.shape[1]
    tm = min(m, DENSE_ROWS)
    assert m % tm == 0 and n % tn == 0
    small_in, small_args, out_specs, out_shape = [], [], pl.BlockSpec((tm, n), lambda i: (i, 0)), \
        jax.ShapeDtypeStruct((m, n), out_dtype)
    if w_small is not None:
        ns = w_small.shape[1]
        small_in, small_args = [_resident((d, ns))], [w_small]
        out_specs = [out_specs, pl.BlockSpec((tm, ns), lambda i: (i, 0))]
        out_shape = [out_shape, jax.ShapeDtypeStruct((m, ns), F32)]
    if rot is None:
        cos = sin = jnp.zeros((SUBLANES, LANES), F32)
        rot_spec = pl.BlockSpec((SUBLANES, LANES), lambda i: (0, 0))
        n_rot = 0
    else:
        cos, sin = rot
        period = cos.shape[0] // tm
        assert cos.shape[0] % tm == 0 and tn == RET_QK
        rot_spec = pl.BlockSpec((tm, LANES), lambda i: (i % period, 0))
        n_rot = 2
    kern = functools.partial(_proj_kernel, tn=tn, n_rot=n_rot, k_scale=RET_DK ** -0.5)
    return pl.pallas_call(
        kern,
        grid=(m // tm,),
        in_specs=[
            pl.BlockSpec((tm, d), lambda i: (i, 0)),
            _resident((1, d)),
            _resident((d, n)),
            rot_spec, rot_spec,
        ] + small_in,
        out_specs=out_specs,
        out_shape=out_shape,
        compiler_params=_params("parallel"),
        name="rms_proj",
    )(x, norm_w.reshape(1, d), w_bf16, cos, sin, *small_args)


MLP_FF_TILE = 1024


def _mix_mlp_kernel(y_ref, wo_ref, x_ref, nw_ref, wu_ref, wd_ref, fw_ref, o_ref, *, final):
    x1 = x_ref[...] + _dot(y_ref[...], wo_ref[...])
    h = (_rms_rows(x1) * nw_ref[...]).astype(BF16)
    acc = x1
    for f in range(wu_ref.shape[1] // MLP_FF_TILE):
        cols = slice(f * MLP_FF_TILE, (f + 1) * MLP_FF_TILE)
        u = _dot(h, wu_ref[:, cols])
        acc = acc + _dot(jnp.square(jnp.maximum(u, 0.0)).astype(BF16), wd_ref[cols, :])
    if final:
        acc = _rms_rows(acc) * fw_ref[...]
    o_ref[...] = acc


def _mix_mlp(y, wo_bf16, x, norm_w, wu_bf16, wd_bf16, final_w=None):
    m, d = x.shape
    k = y.shape[1]
    ff = wu_bf16.shape[1]
    tm = min(m, DENSE_ROWS)
    final = final_w is not None
    fw = (final_w if final else jnp.zeros((d,), F32)).reshape(1, d)
    return pl.pallas_call(
        functools.partial(_mix_mlp_kernel, final=final),
        grid=(m // tm,),
        in_specs=[
            pl.BlockSpec((tm, k), lambda i: (i, 0)),
            _resident((k, d)),
            pl.BlockSpec((tm, d), lambda i: (i, 0)),
            _resident((1, d)),
            _resident((d, ff)),
            _resident((ff, d)),
            _resident((1, d)),
        ],
        out_specs=pl.BlockSpec((tm, d), lambda i: (i, 0)),
        out_shape=jax.ShapeDtypeStruct((m, d), F32),
        compiler_params=_params("parallel"),
        name="mix_mlp",
    )(y, wo_bf16, x, norm_w.reshape(1, d), wu_bf16, wd_bf16, fw)


def _ret_log_gamma(h):
    return math.log1p(-(2.0 ** (-5.0 - h)))


def _ret_decay(rows, seq_len, h):
    lg = _ret_log_gamma(h)
    t = lax.broadcasted_iota(jnp.int32, (rows, rows), 0)
    s = lax.broadcasted_iota(jnp.int32, (rows, rows), 1)
    incl, _ = _seq_masks(rows, seq_len)
    d = jnp.where(incl, jnp.exp(jnp.where(incl, (t - s).astype(F32) * lg, 0.0)), 0.0)
    pos = (lax.broadcasted_iota(jnp.int32, (rows, 1), 0) & (seq_len - 1)).astype(F32)
    return d, pos, lg


def _ret_gate_norm(y, g):
    return jax.nn.silu(g.astype(F32)) * _rms_rows(y)


def _ret_prompt_kernel(q_ref, k_ref, v_ref, g_ref, y_ref, so_ref, s_ref, d_ref):
    c = pl.program_id(1)
    rows = q_ref.shape[0]

    @pl.when(c == 0)
    def _():
        s_ref[...] = jnp.zeros(s_ref.shape, F32)
        for h in range(RET_HEADS):
            d_ref[h] = _ret_decay(rows, rows, h)[0]

    pos = lax.broadcasted_iota(jnp.int32, (rows, 1), 0).astype(F32)
    for h in range(RET_HEADS):
        lg = _ret_log_gamma(h)
        q = q_ref[:, h * RET_DK:(h + 1) * RET_DK]
        k = k_ref[:, h * RET_DK:(h + 1) * RET_DK]
        v = v_ref[:, h * RET_DV:(h + 1) * RET_DV]
        a = (_dot_nt(q, k) * d_ref[h]).astype(BF16)
        s_old = s_ref[h]
        y = _dot(a, v) + jnp.exp((pos + 1.0) * lg) * _dot(q, s_old.astype(BF16))
        kw = (k.astype(F32) * jnp.exp((rows - 1.0 - pos) * lg)).astype(BF16)
        s_ref[h] = s_old * math.exp(rows * lg) + _dot_tn(kw, v)
        y_ref[:, h * RET_DV:(h + 1) * RET_DV] = _ret_gate_norm(
            y, g_ref[:, h * RET_DV:(h + 1) * RET_DV]).astype(y_ref.dtype)

    @pl.when(c == pl.num_programs(1) - 1)
    def _():
        so_ref[0] = s_ref[...]


def _without_arg(fn, idx):
    def wrapped(*refs, **kw):
        return fn(*refs[:idx], *refs[idx + 1:], **kw)
    return wrapped


def _stacked_state_out(n_layers, layer, batch, carried, n_in, index_of, block=(1, RET_HEADS, RET_DK, RET_DV)):
    spec = pl.BlockSpec((None,) + block, lambda *ix: (layer, index_of(*ix), 0, 0, 0))
    shape = jax.ShapeDtypeStruct((n_layers, batch) + block[1:], F32)
    extra_in = [] if carried is None else [pl.BlockSpec(memory_space=pl.ANY)]
    extra_args = [] if carried is None else [carried]
    aliases = {} if carried is None else {n_in: 1}
    return spec, shape, extra_in, extra_args, aliases


def _ret_prompt(proj, batch, seq, layer, n_layers, carried):
    t = min(RET_CHUNK, seq)
    nc = seq // t
    row = lambda b, c: b * nc + c
    n_in = 4
    s_spec, s_shape, extra_in, extra_args, aliases = _stacked_state_out(
        n_layers, layer, batch, carried, n_in, lambda b, c: b)
    kern = _ret_prompt_kernel if carried is None else _without_arg(_ret_prompt_kernel, n_in)
    return pl.pallas_call(
        kern,
        grid=(batch, nc),
        in_specs=[
            pl.BlockSpec((t, RET_QK), lambda b, c: (row(b, c), 0)),
            pl.BlockSpec((t, RET_QK), lambda b, c: (row(b, c), 1)),
            pl.BlockSpec((t, RET_V), lambda b, c: (row(b, c), 1)),
            pl.BlockSpec((t, RET_V), lambda b, c: (row(b, c), 2)),
        ] + extra_in,
        out_specs=[pl.BlockSpec((t, RET_V), lambda b, c: (row(b, c), 0)), s_spec],
        out_shape=[jax.ShapeDtypeStruct((batch * seq, RET_V), BF16), s_shape],
        scratch_shapes=[pltpu.VMEM((RET_HEADS, RET_DK, RET_DV), F32),
                        pltpu.VMEM((RET_HEADS, t, t), F32)],
        input_output_aliases=aliases,
        compiler_params=_params("parallel", "arbitrary"),
        name="ret_prompt",
    )(proj, proj, proj, proj, *extra_args)


def _ret_sample_kernel(q_ref, k_ref, v_ref, g_ref, s0_ref, y_ref, so_ref, acc_ref, *, seq_len):
    j = pl.program_id(1)
    rows = q_ref.shape[0]
    row_seq = lax.broadcasted_iota(jnp.int32, (rows, 1), 0) >> _log2(seq_len)
    mine = row_seq == j

    for h in range(RET_HEADS):
        d, pos, lg = _ret_decay(rows, seq_len, h)
        q = q_ref[:, h * RET_DK:(h + 1) * RET_DK].astype(BF16)
        k = k_ref[:, h * RET_DK:(h + 1) * RET_DK]
        v = v_ref[:, h * RET_DV:(h + 1) * RET_DV].astype(BF16)
        cols = slice(h * RET_DV, (h + 1) * RET_DV)

        @pl.when(j == 0)
        def _():
            a = (_dot_nt(q, k.astype(BF16)) * d).astype(BF16)
            acc_ref[:, cols] = _dot(a, v)

        s_old = s0_ref[0, h]
        y_state = jnp.exp((pos + 1.0) * lg) * _dot(q, s_old.astype(BF16))
        acc_ref[:, cols] += jnp.where(mine, y_state, 0.0)
        kw = jnp.where(mine, k.astype(F32) * jnp.exp((seq_len - 1.0 - pos) * lg), 0.0).astype(BF16)
        so_ref[0, h] = s_old * math.exp(seq_len * lg) + _dot_tn(kw, v)

    @pl.when(j == pl.num_programs(1) - 1)
    def _():
        for h in range(RET_HEADS):
            cols = slice(h * RET_DV, (h + 1) * RET_DV)
            y_ref[:, cols] = _ret_gate_norm(acc_ref[:, cols], g_ref[:, cols]).astype(y_ref.dtype)


def _ret_sample(proj, states, seq_len, layer, carried):
    m = proj.shape[0]
    rows = SAMPLE_TILE
    per_tile = rows // seq_len
    n_layers, batch = states.shape[:2]
    seq_of = lambda i, j: i * per_tile + j
    n_in = 5
    s_spec, s_shape, extra_in, extra_args, aliases = _stacked_state_out(
        n_layers, layer, batch, carried, n_in, seq_of)
    kern = functools.partial(_ret_sample_kernel, seq_len=seq_len)
    if carried is not None:
        kern = _without_arg(kern, n_in)
    return pl.pallas_call(
        kern,
        grid=(m // rows, per_tile),
        in_specs=[
            pl.BlockSpec((rows, RET_QK), lambda i, j: (i, 0)),
            pl.BlockSpec((rows, RET_QK), lambda i, j: (i, 1)),
            pl.BlockSpec((rows, RET_V), lambda i, j: (i, 1)),
            pl.BlockSpec((rows, RET_V), lambda i, j: (i, 2)),
            pl.BlockSpec((None, 1, RET_HEADS, RET_DK, RET_DV), lambda i, j: (layer, seq_of(i, j), 0, 0, 0)),
        ] + extra_in,
        out_specs=[pl.BlockSpec((rows, RET_V), lambda i, j: (i, 0)), s_spec],
        out_shape=[jax.ShapeDtypeStruct((m, RET_V), BF16), s_shape],
        scratch_shapes=[pltpu.VMEM((rows, RET_V), F32)],
        input_output_aliases=aliases,
        compiler_params=_params("parallel", "arbitrary"),
        name="ret_sample",
    )(proj, proj, proj, proj, states, *extra_args)


def _rope_tables(seq, pos0, reps):
    half = RET_DK // 2
    inv = ROPE_BASE ** (-jnp.arange(half, dtype=F32) / half)
    ang = (jnp.arange(seq, dtype=F32) + pos0)[:, None] * inv[None, :]
    return jnp.tile(jnp.cos(ang), (reps, 1)), jnp.tile(jnp.sin(ang), (reps, 1))


def _ret_layer(x, states, carried, layer, n_layers, norm_w, w_in, *, batch, seq, pos0, sample):
    m = x.shape[0]
    if sample:
        rot = _rope_tables(seq, pos0, min(m, DENSE_ROWS) // seq)
    else:
        rot = _rope_tables(seq, pos0, 1)
    proj = _rms_proj(x, norm_w, w_in, None, out_dtype=F32 if sample else BF16, tn=RET_QK, rot=rot)
    if sample:
        y, stack = _ret_sample(proj, states, seq, layer, carried)
    else:
        y, stack = _ret_prompt(proj, batch, seq, layer, n_layers, carried)
    return y, stack


def _conv_chunk(x, prev, cw_ref, cols):
    def taps(src, lo, hi):
        acc = None
        for tap in range(CONV_W):
            back = CONV_W - 1 - tap
            cur = (src if back == 0 else pltpu.roll(src, back, 0))[lo:hi] * cw_ref[tap:tap + 1, cols]
            acc = cur if acc is None else acc + cur
        return acc

    rows = x.shape[0]
    head = taps(jnp.concatenate([prev, x[0:HALO]], axis=0), HALO, 2 * HALO)
    return jnp.concatenate([head, taps(x, HALO, rows)], axis=0)


def _conv_chunk_mxu(x16, prev, cw_ref):
    rows = x16.shape[0]
    ti = lax.broadcasted_iota(jnp.int32, (rows, rows), 0)
    si = lax.broadcasted_iota(jnp.int32, (rows, rows), 1)
    shifts = jnp.concatenate(
        [jnp.where(ti - si == CONV_W - 1 - tap, 1.0, 0.0) for tap in range(CONV_W - 1)], axis=0).astype(BF16)
    moved = _dot(shifts, x16)
    x = x16.astype(F32)
    acc = moved[0:rows] * cw_ref[0:1, :]
    for tap in range(1, CONV_W - 1):
        acc = acc + moved[tap * rows:(tap + 1) * rows] * cw_ref[tap:tap + 1, :]
    acc = acc + x * cw_ref[CONV_W - 1:CONV_W, :]
    window = jnp.concatenate([prev, x[0:HALO]], axis=0)
    head = None
    for tap in range(CONV_W):
        back = CONV_W - 1 - tap
        cur = (window if back == 0 else pltpu.roll(window, back, 0))[HALO:2 * HALO] * cw_ref[tap:tap + 1, :]
        head = cur if head is None else head + cur
    return jnp.concatenate([head, acc[HALO:rows]], axis=0)


def _last_rows(x_ref):
    rows = x_ref.shape[0]
    return x_ref[rows - 2 * HALO:rows, :].astype(F32)[HALO:2 * HALO]


def _conv_packed(u, hist, cw_ref, cols, seq_len):
    rows = u.shape[0]
    pos = lax.broadcasted_iota(jnp.int32, (rows, 1), 0) & (seq_len - 1)
    acc = None
    for tap in range(CONV_W):
        back = CONV_W - 1 - tap
        cur = u if back == 0 else pltpu.roll(u, back, 0)
        if tap < CONV_W - 1:
            old = hist if tap == 0 else pltpu.roll(hist, rows - tap, 0)
            cur = jnp.where(pos + tap <= CONV_W - 2, old, cur)
        term = cur * cw_ref[tap:tap + 1, cols]
        acc = term if acc is None else acc + term
    return acc


def _last_of_seq(x, seq_len):
    rows = x.shape[0]
    if seq_len == rows:
        return jnp.broadcast_to(x[rows - 1:rows, :], x.shape)
    pos = lax.broadcasted_iota(jnp.int32, x.shape, 0) & (seq_len - 1)
    out = x
    for ahead in range(1, seq_len):
        out = jnp.where(pos == seq_len - 1 - ahead, pltpu.roll(x, rows - ahead, 0), out)
    return out


def _pad_lanes(v, offset=0):
    return jnp.pad(v.astype(F32), (offset, LANES - offset - v.shape[0])).reshape(1, LANES)


def _packed_history(buf, seq_len):
    b, r, c = buf.shape
    return jnp.pad(buf, ((0, 0), (0, seq_len - r), (0, 0))).reshape(b * seq_len, c)


def _ssd_expand(x, g):
    rows = x.shape[0]
    head = lax.broadcasted_iota(jnp.int32, (rows, SSD_GROUP_INNER), 1) >> _log2(SSD_HEADDIM)
    h0 = g * SSD_GROUP_HEADS
    out = jnp.broadcast_to(x[:, h0:h0 + 1], (rows, SSD_GROUP_INNER))
    for r in range(1, SSD_GROUP_HEADS):
        out = jnp.where(head == r, jnp.broadcast_to(x[:, h0 + r:h0 + r + 1], (rows, SSD_GROUP_INNER)), out)
    return out


def _ssd_step_terms(dt_ref, dtb_ref, alog_ref, seq_len):
    dt = jax.nn.softplus(dt_ref[...] + dtb_ref[...])
    la = dt * (-jnp.exp(alog_ref[...]))
    ct = _cumsum_rows(la, _seq_masks(la.shape[0], seq_len)[0])
    return dt, ct


def _ssd_group_intra(g, xs, bg, cg, dt, ct, ct_t, incl):
    rows = xs.shape[0]
    scores = _dot_nt(cg, bg)
    xdt = xs * _ssd_expand(dt, g)
    head = lax.broadcasted_iota(jnp.int32, (rows, SSD_GROUP_INNER), 1) >> _log2(SSD_HEADDIM)
    y = jnp.zeros((rows, SSD_GROUP_INNER), F32)
    for r in range(SSD_GROUP_HEADS):
        h = g * SSD_GROUP_HEADS + r
        diff = ct[:, h:h + 1] - ct_t[h:h + 1, :]
        d = jnp.where(incl, jnp.exp(jnp.where(incl, diff, 0.0)), 0.0)
        a = (scores * d).astype(BF16)
        y = y + _dot(a, jnp.where(head == r, xdt, 0.0).astype(BF16))
    return y, xdt, _ssd_expand(ct, g)


def _ssd_gate_norm(y, z, nw):
    return _rms_rows(y * jax.nn.silu(z.astype(F32))) * nw


def _ssd_prompt_kernel(z_ref, x_ref, bc_ref, dt_ref, cwx_ref, cwbc_ref, cbx_ref, cbbc_ref, dtb_ref, alog_ref,
                       dsk_ref, nw_ref, y_ref, so_ref, px_ref, pbc_ref, st_ref):
    c = pl.program_id(1)
    rows = x_ref.shape[0]
    gs = SSD_GROUPS * SSD_STATE

    @pl.when(c == 0)
    def _():
        px_ref[...] = jnp.zeros(px_ref.shape, F32)
        pbc_ref[...] = jnp.zeros(pbc_ref.shape, F32)
        st_ref[...] = jnp.zeros(st_ref.shape, F32)

    dt, ct = _ssd_step_terms(dt_ref, dtb_ref, alog_ref, rows)
    ct_t = ct.T
    incl, _ = _seq_masks(rows, rows)
    xs_all = jax.nn.silu(_conv_chunk_mxu(x_ref[...], px_ref[...], cwx_ref) + cbx_ref[...])
    bc_all = jax.nn.silu(_conv_chunk_mxu(bc_ref[...], pbc_ref[...], cwbc_ref) + cbbc_ref[...]).astype(BF16)

    for g in range(SSD_GROUPS):
        ci = slice(g * SSD_GROUP_INNER, (g + 1) * SSD_GROUP_INNER)
        xs = xs_all[:, ci]
        bg = bc_all[:, g * SSD_STATE:(g + 1) * SSD_STATE]
        cg = bc_all[:, gs + g * SSD_STATE:gs + (g + 1) * SSD_STATE]
        y, xdt, ctc = _ssd_group_intra(g, xs, bg, cg, dt, ct, ct_t, incl)
        st = st_ref[g]
        y = y + jnp.exp(ctc) * _dot(cg, st.astype(BF16)) + dsk_ref[:, ci] * xs
        last = ctc[rows - 1:rows, :]
        st_ref[g] = st * jnp.exp(last) + _dot_tn(bg, (xdt * jnp.exp(last - ctc)).astype(BF16))
        y_ref[:, ci] = _ssd_gate_norm(y, z_ref[:, ci], nw_ref[:, ci]).astype(y_ref.dtype)

    px_ref[...] = _last_rows(x_ref)
    pbc_ref[...] = _last_rows(bc_ref)

    @pl.when(c == pl.num_programs(1) - 1)
    def _():
        for g in range(SSD_GROUPS):
            hs = slice(g * SSD_GROUP_HEADS, (g + 1) * SSD_GROUP_HEADS)
            so_ref[0, hs] = st_ref[g].T.reshape(SSD_GROUP_HEADS, SSD_HEADDIM, SSD_STATE)


def _ssd_prompt(proj, dtp, prm, batch, seq):
    t = min(SSD_CHUNK, seq)
    nc = seq // t
    row = lambda b, c: b * nc + c
    half = SSD_INNER
    full = lambda shape: pl.BlockSpec(shape, lambda b, c: (0,) * len(shape))
    return pl.pallas_call(
        _ssd_prompt_kernel,
        grid=(batch, nc),
        in_specs=[
            pl.BlockSpec((t, half), lambda b, c: (row(b, c), 0)),
            pl.BlockSpec((t, half), lambda b, c: (row(b, c), 1)),
            pl.BlockSpec((t, half), lambda b, c: (row(b, c), 2)),
            pl.BlockSpec((t, LANES), lambda b, c: (row(b, c), 0)),
            pl.BlockSpec((CONV_W, half), lambda b, c: (0, 0)),
            pl.BlockSpec((CONV_W, half), lambda b, c: (0, 1)),
            pl.BlockSpec((1, half), lambda b, c: (0, 0)),
            pl.BlockSpec((1, half), lambda b, c: (0, 1)),
            full((1, LANES)), full((1, LANES)), full((1, half)), full((1, half)),
        ],
        out_specs=[
            pl.BlockSpec((t, half), lambda b, c: (row(b, c), 0)),
            pl.BlockSpec((1, SSD_HEADS, SSD_HEADDIM, SSD_STATE), lambda b, c: (b, 0, 0, 0)),
        ],
        out_shape=[
            jax.ShapeDtypeStruct((batch * seq, half), BF16),
            jax.ShapeDtypeStruct((batch, SSD_HEADS, SSD_HEADDIM, SSD_STATE), F32),
        ],
        scratch_shapes=[
            pltpu.VMEM((HALO, half), F32),
            pltpu.VMEM((HALO, half), F32),
            pltpu.VMEM((SSD_GROUPS, SSD_STATE, SSD_GROUP_INNER), F32),
        ],
        compiler_params=_params("parallel", "arbitrary"),
        name="ssd_prompt",
    )(proj, proj, proj, dtp, prm["cw"], prm["cw"], prm["cb"], prm["cb"], prm["dtb"], prm["alog"],
      prm["dsk"], prm["nw"])


SSD_SAMPLE_SEQS = 4


def _ssd_sample_kernel(z_ref, x_ref, bc_ref, hx_ref, hbc_ref, dt_ref, cwx_ref, cwbc_ref, cbx_ref, cbbc_ref,
                       dtb_ref, alog_ref, dsk_ref, nw_ref, s0_ref, y_ref, so_ref,
                       acc_ref, ect_ref, xdw_ref, b_ref, c_ref, ct_ref, *, seq_len):
    j = pl.program_id(1)
    rows = x_ref.shape[0]
    gs = SSD_GROUPS * SSD_STATE

    @pl.when(j == 0)
    def _():
        dt, ct = _ssd_step_terms(dt_ref, dtb_ref, alog_ref, seq_len)
        ct_ref[...] = ct
        ct_t = ct.T
        incl, _ = _seq_masks(rows, seq_len)
        for g in range(SSD_GROUPS):
            ci = slice(g * SSD_GROUP_INNER, (g + 1) * SSD_GROUP_INNER)
            cb_ = slice(g * SSD_STATE, (g + 1) * SSD_STATE)
            cc_ = slice(gs + g * SSD_STATE, gs + (g + 1) * SSD_STATE)
            xs = jax.nn.silu(_conv_packed(x_ref[:, ci], hx_ref[:, ci], cwx_ref, ci, seq_len) + cbx_ref[:, ci])
            bg = jax.nn.silu(_conv_packed(bc_ref[:, cb_], hbc_ref[:, cb_], cwbc_ref, cb_, seq_len)
                             + cbbc_ref[:, cb_]).astype(BF16)
            cg = jax.nn.silu(_conv_packed(bc_ref[:, cc_], hbc_ref[:, cc_], cwbc_ref, cc_, seq_len)
                             + cbbc_ref[:, cc_]).astype(BF16)
            y, xdt, ctc = _ssd_group_intra(g, xs, bg, cg, dt, ct, ct_t, incl)
            acc_ref[:, ci] = y + dsk_ref[:, ci] * xs
            ect_ref[:, ci] = jnp.exp(ctc)
            xdw_ref[:, ci] = xdt * jnp.exp(_last_of_seq(ctc, seq_len) - ctc)
            b_ref[:, cb_] = bg
            c_ref[:, cb_] = cg

    row_seq = lax.broadcasted_iota(jnp.int32, (rows, 1), 0) >> _log2(seq_len)
    for n in range(SSD_SAMPLE_SEQS):
        seq = j * SSD_SAMPLE_SEQS + n
        mine = row_seq == seq
        last_row = seq * seq_len + (seq_len - 1)
        for g in range(SSD_GROUPS):
            ci = slice(g * SSD_GROUP_INNER, (g + 1) * SSD_GROUP_INNER)
            cb_ = slice(g * SSD_STATE, (g + 1) * SSD_STATE)
            hs = slice(g * SSD_GROUP_HEADS, (g + 1) * SSD_GROUP_HEADS)
            sg = s0_ref[n, hs].reshape(SSD_GROUP_INNER, SSD_STATE)
            y_state = ect_ref[:, ci] * _dot_nt(c_ref[:, cb_], sg.astype(BF16))
            acc_ref[:, ci] += jnp.where(mine, y_state, 0.0)
            upd = _dot_tn(jnp.where(mine, xdw_ref[:, ci], 0.0).astype(BF16), b_ref[:, cb_])
            for r in range(SSD_GROUP_HEADS):
                h = g * SSD_GROUP_HEADS + r
                keep = jnp.exp(ct_ref[pl.ds(last_row, 1), pl.ds(h, 1)])
                so_ref[n, h] = s0_ref[n, h] * keep + upd[r * SSD_HEADDIM:(r + 1) * SSD_HEADDIM, :]

    @pl.when(j == pl.num_programs(1) - 1)
    def _():
        for g in range(SSD_GROUPS):
            ci = slice(g * SSD_GROUP_INNER, (g + 1) * SSD_GROUP_INNER)
            y_ref[:, ci] = _ssd_gate_norm(acc_ref[:, ci], z_ref[:, ci], nw_ref[:, ci]).astype(y_ref.dtype)


def _ssd_sample(proj, dtp, hist, prm, s0, seq_len):
    m = proj.shape[0]
    rows = SAMPLE_TILE
    nb = SSD_SAMPLE_SEQS
    steps = rows // seq_len // nb
    half = SSD_INNER
    full = lambda shape: pl.BlockSpec(shape, lambda i, j: (0,) * len(shape))
    state_spec = pl.BlockSpec((nb, SSD_HEADS, SSD_HEADDIM, SSD_STATE), lambda i, j: (i * steps + j, 0, 0, 0))
    return pl.pallas_call(
        functools.partial(_ssd_sample_kernel, seq_len=seq_len),
        grid=(m // rows, steps),
        in_specs=[
            pl.BlockSpec((rows, half), lambda i, j: (i, 0)),
            pl.BlockSpec((rows, half), lambda i, j: (i, 1)),
            pl.BlockSpec((rows, half), lambda i, j: (i, 2)),
            pl.BlockSpec((rows, half), lambda i, j: (i, 0)),
            pl.BlockSpec((rows, half), lambda i, j: (i, 1)),
            pl.BlockSpec((rows, LANES), lambda i, j: (i, 0)),
            pl.BlockSpec((CONV_W, half), lambda i, j: (0, 0)),
            pl.BlockSpec((CONV_W, half), lambda i, j: (0, 1)),
            pl.BlockSpec((1, half), lambda i, j: (0, 0)),
            pl.BlockSpec((1, half), lambda i, j: (0, 1)),
            full((1, LANES)), full((1, LANES)), full((1, half)), full((1, half)),
            state_spec,
        ],
        out_specs=[pl.BlockSpec((rows, half), lambda i, j: (i, 0)), state_spec],
        out_shape=[
            jax.ShapeDtypeStruct((m, half), BF16),
            jax.ShapeDtypeStruct(s0.shape, F32),
        ],
        scratch_shapes=[
            pltpu.VMEM((rows, half), F32),
            pltpu.VMEM((rows, half), F32),
            pltpu.VMEM((rows, half), F32),
            pltpu.VMEM((rows, SSD_GROUPS * SSD_STATE), BF16),
            pltpu.VMEM((rows, SSD_GROUPS * SSD_STATE), BF16),
            pltpu.VMEM((rows, LANES), F32),
        ],
        compiler_params=_params("parallel", "arbitrary"),
        name="ssd_sample",
    )(proj, proj, proj, hist, hist, dtp, prm["cw"], prm["cw"], prm["cb"], prm["cb"], prm["dtb"], prm["alog"],
      prm["dsk"], prm["nw"], s0)


def _ssd_layer(x, s0, cbuf, norm_w, w_main, w_dt, prm, *, batch, seq, sample):
    proj, dtp = _rms_proj(x, norm_w, w_main, w_dt, out_dtype=F32 if sample else BF16, tn=1024)
    conv_new = proj.reshape(batch, seq, -1)[:, seq - (CONV_W - 1):, SSD_INNER:].astype(F32)
    if sample:
        y, s_new = _ssd_sample(proj, dtp, _packed_history(cbuf, seq), prm, s0, seq)
    else:
        y, s_new = _ssd_prompt(proj, dtp, prm, batch, seq)
    return y, s_new, conv_new


def _l2norm_rows(x):
    return x * lax.rsqrt(jnp.sum(x * x, axis=-1, keepdims=True) + RMS_EPS)


def _gdn_gates(ba_ref, dtb_ref, alog_ref, seq_len):
    ba = ba_ref[...]
    beta = jax.nn.sigmoid(ba)
    g = -jnp.exp(alog_ref[...]) * jax.nn.softplus(ba + dtb_ref[...])
    gc = _cumsum_rows(g, _seq_masks(g.shape[0], seq_len)[0])
    return beta, gc, _last_of_seq(gc, seq_len)


INV_BASE_BLOCK = 16
GDN_PROMPT_KEY_HEADS = 8


def _unit_lower_inverse(ls, seq_len):
    rows = ls[0].shape[0]
    ti = lax.broadcasted_iota(jnp.int32, (rows, rows), 0)
    si = lax.broadcasted_iota(jnp.int32, (rows, rows), 1)
    base = min(INV_BASE_BLOCK, seq_len)
    diag_block = (ti >> _log2(base)) == (si >> _log2(base))
    ns = [jnp.where(diag_block, -l, 0.0) for l in ls]
    eye = jnp.where(ti == si, 1.0, 0.0)
    xs = [eye + n for n in ns]
    for _ in range(_log2(base) - 1):
        n16s = [n.astype(BF16) for n in ns]
        ns = [_dot(n16, n16) for n16 in n16s]
        xs = [x + _dot(x.astype(BF16), n.astype(BF16)) for x, n in zip(xs, ns)]
    blk = base
    while blk < seq_len:
        lower_left = ((ti >> _log2(2 * blk)) == (si >> _log2(2 * blk))) & ((ti >> _log2(blk)) != (si >> _log2(blk)))
        cs = [jnp.where(lower_left, l, 0.0).astype(BF16) for l in ls]
        x16s = [x.astype(BF16) for x in xs]
        ts = [_dot(x16, c).astype(BF16) for x16, c in zip(x16s, cs)]
        xs = [x - _dot(t, x16) for x, t, x16 in zip(xs, ts, x16s)]
        blk *= 2
    return xs


def _gdn_heads_intra(qs, ks, kks, qks, vs, beta_all, gc_all, gl_all, gc_t, heads, incl, strict, seq_len):
    rows = qs[0].shape[0]
    betas, gccs, gcls, ds = [], [], [], []
    for head in heads:
        if isinstance(head, int):
            beta = beta_all[:, head:head + 1]
            gcc = gc_all[:, GDN_VH + head:GDN_VH + head + 1]
            gcl = gl_all[:, GDN_VH + head:GDN_VH + head + 1]
            gcr = gc_t[GDN_VH + head:GDN_VH + head + 1, :]
        else:
            lane = lax.broadcasted_iota(jnp.int32, (rows, LANES), 1)
            sub = lax.broadcasted_iota(jnp.int32, (LANES, rows), 0)
            beta = jnp.sum(jnp.where(lane == head, beta_all, 0.0), axis=1, keepdims=True)
            gcc = jnp.sum(jnp.where(lane == GDN_VH + head, gc_all, 0.0), axis=1, keepdims=True)
            gcl = jnp.sum(jnp.where(lane == GDN_VH + head, gl_all, 0.0), axis=1, keepdims=True)
            gcr = jnp.sum(jnp.where(sub == GDN_VH + head, gc_t, 0.0), axis=0, keepdims=True)
        betas.append(beta)
        gccs.append(gcc)
        gcls.append(gcl)
        ds.append(jnp.where(incl, jnp.exp(jnp.where(incl, gcc - gcr, 0.0)), 0.0))
    key = lambda i: i // GDN_REP
    n = len(heads)
    ps = _unit_lower_inverse(
        [jnp.where(strict, (betas[i] * kks[key(i)]) * ds[i], 0.0) for i in range(n)], seq_len)
    egcs = [jnp.exp(g) for g in gccs]
    rhss = [jnp.concatenate([vs[i] * betas[i], ks[key(i)] * (betas[i] * egcs[i])], axis=1).astype(BF16)
            for i in range(n)]
    sols = [_dot(ps[i].astype(BF16), rhss[i]) for i in range(n)]
    us = [s[:, :GDN_DV] for s in sols]
    ws = [s[:, GDN_DV:].astype(BF16) for s in sols]
    attns = [(qks[key(i)] * ds[i]).astype(BF16) for i in range(n)]
    qds = [(qs[key(i)] * egcs[i]).astype(BF16) for i in range(n)]
    kds = [(ks[key(i)] * jnp.exp(gcls[i] - gccs[i])).astype(BF16) for i in range(n)]
    return us, ws, attns, qds, kds, gccs


def _gdn_gate_norm(o, z, nw):
    return _rms_rows(o) * nw * jax.nn.silu(z.astype(F32))


def _gdn_prompt_kernel(q_ref, k_ref, v_ref, z_ref, ba_ref, cwq_ref, cwk_ref, cwv_ref, dtb_ref, alog_ref, nw_ref,
                       o_ref, so_ref, pq_ref, pk_ref, pv_ref, s_ref):
    c = pl.program_id(1)
    rows = q_ref.shape[0]

    @pl.when(c == 0)
    def _():
        for prev in (pq_ref, pk_ref, pv_ref):
            prev[...] = jnp.zeros(prev.shape, F32)
        s_ref[...] = jnp.zeros(s_ref.shape, F32)

    beta_all, gc_all, gl_all = _gdn_gates(ba_ref, dtb_ref, alog_ref, rows)
    gc_t = gc_all.T
    incl, strict = _seq_masks(rows, rows)

    for kh0 in range(0, GDN_KH, GDN_PROMPT_KEY_HEADS):
        khs = range(kh0, kh0 + GDN_PROMPT_KEY_HEADS)
        heads = [kh * GDN_REP + r for kh in khs for r in range(GDN_REP)]
        key_cols = [slice(kh * GDN_DK, (kh + 1) * GDN_DK) for kh in khs]
        val_cols = [slice(h * GDN_DV, (h + 1) * GDN_DV) for h in heads]
        qs = [_l2norm_rows(jax.nn.silu(_conv_chunk(q_ref[:, c_].astype(F32), pq_ref[:, c_], cwq_ref, c_)))
              * (GDN_DK ** -0.5) for c_ in key_cols]
        ks = [_l2norm_rows(jax.nn.silu(_conv_chunk(k_ref[:, c_].astype(F32), pk_ref[:, c_], cwk_ref, c_)))
              for c_ in key_cols]
        vs = [jax.nn.silu(_conv_chunk(v_ref[:, c_].astype(F32), pv_ref[:, c_], cwv_ref, c_)) for c_ in val_cols]
        q16s = [q.astype(BF16) for q in qs]
        k16s = [k.astype(BF16) for k in ks]
        kks = [_dot_nt(k16, k16) for k16 in k16s]
        qks = [_dot_nt(q16, k16) for q16, k16 in zip(q16s, k16s)]
        us, ws, attns, qds, kds, gccs = _gdn_heads_intra(
            qs, ks, kks, qks, vs, beta_all, gc_all, gl_all, gc_t, heads, incl, strict, rows)
        n = len(heads)
        s_olds = [s_ref[h] for h in heads]
        s16s = [s.astype(BF16) for s in s_olds]
        v_news = [us[i] - _dot(ws[i], s16s[i]) for i in range(n)]
        vn16s = [v.astype(BF16) for v in v_news]
        os_ = [_dot(qds[i], s16s[i]) + _dot(attns[i], vn16s[i]) for i in range(n)]
        for i, h in enumerate(heads):
            s_ref[h] = s_olds[i] * jnp.exp(gccs[i][rows - 1:rows, :]) + _dot_tn(kds[i], vn16s[i])
            o_ref[:, val_cols[i]] = _gdn_gate_norm(os_[i], z_ref[:, val_cols[i]], nw_ref[...]).astype(o_ref.dtype)

    pq_ref[...] = _last_rows(q_ref)
    pk_ref[...] = _last_rows(k_ref)
    pv_ref[...] = _last_rows(v_ref)

    @pl.when(c == pl.num_programs(1) - 1)
    def _():
        so_ref[0] = s_ref[...]


def _gdn_col_specs(rows, row_map):
    kcol = GDN_KD // GDN_DK
    vcol = 2 * GDN_KD // (GDN_REP * GDN_DV)
    return [
        pl.BlockSpec((rows, GDN_DK), lambda *ix: (row_map(*ix), ix[1])),
        pl.BlockSpec((rows, GDN_DK), lambda *ix: (row_map(*ix), kcol + ix[1])),
        pl.BlockSpec((rows, GDN_REP * GDN_DV), lambda *ix: (row_map(*ix), vcol + ix[1])),
    ]


def _gdn_prompt(proj, ba, prm, batch, seq):
    t = min(GDN_CHUNK, seq)
    nc = seq // t
    row = lambda b, c: b * nc + c
    full = lambda shape: pl.BlockSpec(shape, lambda b, c: (0,) * len(shape))
    return pl.pallas_call(
        _gdn_prompt_kernel,
        grid=(batch, nc),
        in_specs=[
            pl.BlockSpec((t, GDN_KD), lambda b, c: (row(b, c), 0)),
            pl.BlockSpec((t, GDN_KD), lambda b, c: (row(b, c), 1)),
            pl.BlockSpec((t, GDN_V), lambda b, c: (row(b, c), 1)),
            pl.BlockSpec((t, GDN_V), lambda b, c: (row(b, c), 2)),
            pl.BlockSpec((t, LANES), lambda b, c: (row(b, c), 0)),
            pl.BlockSpec((CONV_W, GDN_KD), lambda b, c: (0, 0)),
            pl.BlockSpec((CONV_W, GDN_KD), lambda b, c: (0, 1)),
            pl.BlockSpec((CONV_W, GDN_V), lambda b, c: (0, 1)),
            full((1, LANES)), full((1, LANES)), full((1, GDN_DV)),
        ],
        out_specs=[
            pl.BlockSpec((t, GDN_V), lambda b, c: (row(b, c), 0)),
            pl.BlockSpec((1, GDN_VH, GDN_DK, GDN_DV), lambda b, c: (b, 0, 0, 0)),
        ],
        out_shape=[
            jax.ShapeDtypeStruct((batch * seq, GDN_V), BF16),
            jax.ShapeDtypeStruct((batch, GDN_VH, GDN_DK, GDN_DV), F32),
        ],
        scratch_shapes=[
            pltpu.VMEM((HALO, GDN_KD), F32),
            pltpu.VMEM((HALO, GDN_KD), F32),
            pltpu.VMEM((HALO, GDN_V), F32),
            pltpu.VMEM((GDN_VH, GDN_DK, GDN_DV), F32),
        ],
        compiler_params=_params("parallel", "arbitrary"),
        name="gdn_prompt",
    )(proj, proj, proj, proj, ba, prm["cw"], prm["cw"], prm["cw"], prm["dtb"], prm["alog"], prm["nw"])


GDN_SAMPLE_SEQS = 8


def _gdn_sample_kernel(q_ref, k_ref, v_ref, z_ref, ba_ref, hq_ref, hk_ref, hv_ref, cwq_ref, cwk_ref, cwv_ref,
                       dtb_ref, alog_ref, nw_ref, s0_ref, o_ref, so_ref,
                       u_ref, w_ref, attn_ref, qd_ref, kd_ref, gcc_ref, vnew_ref, qs_ref, *, seq_len):
    kh = pl.program_id(1)
    j = pl.program_id(2)
    rows = q_ref.shape[0]

    @pl.when(j == 0)
    def _():
        one = slice(0, GDN_DK)
        q = _l2norm_rows(jax.nn.silu(_conv_packed(q_ref[...], hq_ref[...], cwq_ref, one, seq_len))) * (GDN_DK ** -0.5)
        k = _l2norm_rows(jax.nn.silu(_conv_packed(k_ref[...], hk_ref[...], cwk_ref, one, seq_len)))
        v_all = jax.nn.silu(_conv_packed(v_ref[...], hv_ref[...], cwv_ref, slice(0, GDN_REP * GDN_DV), seq_len))
        beta_all, gc_all, gl_all = _gdn_gates(ba_ref, dtb_ref, alog_ref, seq_len)
        gc_t = gc_all.T
        incl, strict = _seq_masks(rows, seq_len)
        q16, k16 = q.astype(BF16), k.astype(BF16)
        kk = _dot_nt(k16, k16)
        qk = _dot_nt(q16, k16)
        vs = [v_all[:, r * GDN_DV:(r + 1) * GDN_DV] for r in range(GDN_REP)]
        heads = [kh * GDN_REP + r for r in range(GDN_REP)]
        us, ws, attns, qds, kds, gccs = _gdn_heads_intra(
            [q], [k], [kk], [qk], vs, beta_all, gc_all, gl_all, gc_t, heads, incl, strict, seq_len)
        for r in range(GDN_REP):
            u_ref[r] = us[r]
            w_ref[r] = ws[r]
            attn_ref[r] = attns[r]
            qd_ref[r] = qds[r]
            kd_ref[r] = kds[r]
            gcc_ref[r] = jnp.broadcast_to(gccs[r], (rows, LANES))

    win = GDN_SAMPLE_SEQS * seq_len
    row0 = pl.multiple_of(j * win, win)
    in_win = pl.ds(row0, win)
    win_seq = lax.broadcasted_iota(jnp.int32, (win, GDN_DV), 0) >> _log2(seq_len)
    for r in range(GDN_REP):
        u_w, w_w, qd_w = u_ref[r, in_win, :], w_ref[r, in_win, :], qd_ref[r, in_win, :]
        kd_w = kd_ref[r, in_win, :].astype(F32)
        v_new_all = jnp.zeros((win, GDN_DV), F32)
        qs_all = jnp.zeros((win, GDN_DV), F32)
        for n in range(GDN_SAMPLE_SEQS):
            mine = win_seq == n
            s_old = s0_ref[n, r]
            s16 = s_old.astype(BF16)
            v_new = u_w - _dot(w_w, s16)
            v_new_all = jnp.where(mine, v_new, v_new_all)
            qs_all = jnp.where(mine, _dot(qd_w, s16), qs_all)
            kd = jnp.where(mine, kd_w, 0.0).astype(BF16)
            keep = jnp.exp(gcc_ref[r, pl.ds(row0 + n * seq_len + (seq_len - 1), 1), :])
            so_ref[n, r] = s_old * keep + _dot_tn(kd, v_new.astype(BF16))
        vnew_ref[r, in_win, :] = v_new_all
        qs_ref[r, in_win, :] = qs_all

    @pl.when(j == pl.num_programs(2) - 1)
    def _():
        for r in range(GDN_REP):
            cv = slice(r * GDN_DV, (r + 1) * GDN_DV)
            o = qs_ref[r] + _dot(attn_ref[r], vnew_ref[r].astype(BF16))
            o_ref[:, cv] = _gdn_gate_norm(o, z_ref[:, cv], nw_ref[...]).astype(o_ref.dtype)


def _gdn_sample(proj, ba, hist, prm, s0, seq_len):
    m = proj.shape[0]
    rows = SAMPLE_TILE
    nb = GDN_SAMPLE_SEQS
    steps = rows // seq_len // nb
    row_map = lambda i, kh, j: i
    zcol = (2 * GDN_KD + GDN_V) // (GDN_REP * GDN_DV)
    pair = GDN_REP * GDN_DV
    full = lambda shape: pl.BlockSpec(shape, lambda i, kh, j: (0,) * len(shape))
    state_spec = pl.BlockSpec((nb, GDN_REP, GDN_DK, GDN_DV), lambda i, kh, j: (i * steps + j, kh, 0, 0))
    head_f32 = pltpu.VMEM((GDN_REP, rows, GDN_DV), F32)
    head_bf16 = pltpu.VMEM((GDN_REP, rows, GDN_DV), BF16)
    return pl.pallas_call(
        functools.partial(_gdn_sample_kernel, seq_len=seq_len),
        grid=(m // rows, GDN_KH, steps),
        in_specs=_gdn_col_specs(rows, row_map) + [
            pl.BlockSpec((rows, pair), lambda i, kh, j: (i, zcol + kh)),
            pl.BlockSpec((rows, LANES), lambda i, kh, j: (i, 0)),
        ] + _gdn_col_specs(rows, row_map) + _gdn_col_specs(CONV_W, lambda i, kh, j: 0) + [
            full((1, LANES)), full((1, LANES)), full((1, GDN_DV)), state_spec],
        out_specs=[pl.BlockSpec((rows, pair), lambda i, kh, j: (i, kh)), state_spec],
        out_shape=[
            jax.ShapeDtypeStruct((m, GDN_V), BF16),
            jax.ShapeDtypeStruct(s0.shape, F32),
        ],
        scratch_shapes=[head_f32, head_bf16, head_bf16, head_bf16, head_bf16, head_f32, head_f32, head_f32],
        compiler_params=_params("parallel", "parallel", "arbitrary"),
        name="gdn_sample",
    )(proj, proj, proj, proj, ba, hist, hist, hist, prm["cw"], prm["cw"], prm["cw"], prm["dtb"], prm["alog"],
      prm["nw"], s0)


def _gdn_layer(x, s0, cbuf, norm_w, w_main, w_ba, prm, *, batch, seq, sample):
    proj, ba = _rms_proj(x, norm_w, w_main, w_ba, out_dtype=F32 if sample else BF16, tn=1024)
    conv_new = proj.reshape(batch, seq, -1)[:, seq - (CONV_W - 1):, :2 * GDN_KD + GDN_V].astype(F32)
    if sample:
        o, s_new = _gdn_sample(proj, ba, _packed_history(cbuf, seq), prm, s0, seq)
    else:
        o, s_new = _gdn_prompt(proj, ba, prm, batch, seq)
    return o, s_new, conv_new


def _layer_of(stack, j):
    if stack is None:
        return None
    return stack.reshape(stack.shape[1:]) if stack.shape[0] == 1 else stack[j]


def _stack(layers):
    return layers[0].reshape((1,) + layers[0].shape) if len(layers) == 1 else jnp.stack(layers)


def _trunk(x, weights, states, *, batch, seq, pos0, sample):
    s_ret, s_ssd, c_ssd, s_gdn, c_gdn = states
    n_ssd, n_ssdc, n_gdn, n_gdnc = [], [], [], []
    ret_stack = None
    depth = weights["norm_mix"].shape[0]
    n_ret_layers = weights["ret_w_in"].shape[0]
    for i in range(depth):
        kind, jdx = i % 3, i // 3
        nm = weights["norm_mix"][i]
        if kind == 0:
            y, ret_stack = _ret_layer(x, s_ret, ret_stack, jdx, n_ret_layers, nm, weights["ret_w_in"][jdx],
                                      batch=batch, seq=seq, pos0=pos0, sample=sample)
            w_out = weights["ret_w_out"][jdx]
        elif kind == 1:
            y, s, cnew = _ssd_layer(x, _layer_of(s_ssd, jdx), _layer_of(c_ssd, jdx), nm, weights["ssd_w_main"][jdx],
                                    weights["ssd_w_dt"][jdx], weights["ssd_prm"][jdx],
                                    batch=batch, seq=seq, sample=sample)
            w_out = weights["ssd_w_out"][jdx]
            n_ssd.append(s)
            n_ssdc.append(cnew)
        else:
            y, s, cnew = _gdn_layer(x, _layer_of(s_gdn, jdx), _layer_of(c_gdn, jdx), nm, weights["gdn_w_main"][jdx],
                                    weights["gdn_w_ba"][jdx], weights["gdn_prm"][jdx],
                                    batch=batch, seq=seq, sample=sample)
            w_out = weights["gdn_w_out"][jdx]
            n_gdn.append(s)
            n_gdnc.append(cnew)
        final_w = weights["norm_final"] if i == depth - 1 else None
        x = _mix_mlp(y, w_out, x, weights["norm_mlp"][i], weights["mlp_w_up"][i], weights["mlp_w_down"][i], final_w)
    return (x.reshape(batch, seq, -1), ret_stack, _stack(n_ssd), _stack(n_ssdc), _stack(n_gdn), _stack(n_gdnc))


def _prepare_weights(norm_mix, norm_mlp, norm_final, ret_w_in, ret_w_out,
                     ssd_w_in, ssd_conv_w, ssd_conv_b, ssd_dt_bias, ssd_a_log, ssd_d, ssd_norm, ssd_w_out,
                     gdn_w_in, gdn_conv_w, gdn_dt_bias, gdn_a_log, gdn_norm, gdn_w_out, mlp_w_up, mlp_w_down):
    n_ssd, n_gdn = ssd_w_in.shape[0], gdn_w_in.shape[0]
    main = 3 * SSD_INNER

    def small(w):
        return jnp.pad(w[:, main:], ((0, 0), (0, LANES - (w.shape[1] - main)))).astype(BF16)

    ssd_prm = [dict(cw=ssd_conv_w[i], cb=ssd_conv_b[i].reshape(1, -1), dtb=_pad_lanes(ssd_dt_bias[i]),
                    alog=_pad_lanes(ssd_a_log[i]), dsk=jnp.repeat(ssd_d[i], SSD_HEADDIM).reshape(1, -1),
                    nw=ssd_norm[i].reshape(1, -1)) for i in range(n_ssd)]
    gdn_prm = [dict(cw=gdn_conv_w[i], dtb=_pad_lanes(gdn_dt_bias[i], GDN_VH), alog=_pad_lanes(gdn_a_log[i], GDN_VH),
                    nw=gdn_norm[i].reshape(1, -1)) for i in range(n_gdn)]
    return dict(
        norm_mix=norm_mix, norm_mlp=norm_mlp, norm_final=norm_final,
        ret_w_in=ret_w_in.astype(BF16), ret_w_out=ret_w_out.astype(BF16),
        ssd_w_main=[ssd_w_in[i, :, :main].astype(BF16) for i in range(n_ssd)],
        ssd_w_dt=[small(ssd_w_in[i]) for i in range(n_ssd)],
        ssd_prm=ssd_prm, ssd_w_out=ssd_w_out.astype(BF16),
        gdn_w_main=[gdn_w_in[i, :, :main].astype(BF16) for i in range(n_gdn)],
        gdn_w_ba=[small(gdn_w_in[i]) for i in range(n_gdn)],
        gdn_prm=gdn_prm, gdn_w_out=gdn_w_out.astype(BF16),
        mlp_w_up=mlp_w_up.astype(BF16), mlp_w_down=mlp_w_down.astype(BF16),
    )


def kernel(x_prompt, x_sample, state_ret, state_ssd, state_ssd_conv, state_gdn, state_gdn_conv, norm_mix, norm_mlp,
           norm_final, ret_w_in, ret_w_out, ssd_w_in, ssd_conv_w, ssd_conv_b, ssd_dt_bias, ssd_a_log, ssd_d, ssd_norm,
           ssd_w_out, gdn_w_in, gdn_conv_w, gdn_dt_bias, gdn_a_log, gdn_norm, gdn_w_out, mlp_w_up, mlp_w_down):
    weights = _prepare_weights(norm_mix, norm_mlp, norm_final, ret_w_in, ret_w_out, ssd_w_in, ssd_conv_w, ssd_conv_b,
                               ssd_dt_bias, ssd_a_log, ssd_d, ssd_norm, ssd_w_out, gdn_w_in, gdn_conv_w, gdn_dt_bias,
                               gdn_a_log, gdn_norm, gdn_w_out, mlp_w_up, mlp_w_down)
    bp, lp, d = x_prompt.shape
    bs, ls, _ = x_sample.shape
    assert lp % RET_CHUNK == 0 and lp % SSD_CHUNK == 0 and lp % GDN_CHUNK == 0
    assert SAMPLE_TILE % ls == 0 and (bs * ls) % SAMPLE_TILE == 0 and ls >= CONV_W - 1
    prompt = _trunk(x_prompt.reshape(bp * lp, d), weights, (None,) * 5, batch=bp, seq=lp, pos0=0, sample=False)
    sample = _trunk(x_sample.reshape(bs * ls, d), weights,
                    (state_ret, state_ssd, state_ssd_conv, state_gdn, state_gdn_conv),
                    batch=bs, seq=ls, pos0=PAST_LEN, sample=True)
    return (prompt[0], sample[0]) + prompt[1:] + sample[1:]
```
